```python
import jax, jax.numpy as jnp
from jax import lax
import numpy as np

D_MODEL = 2048
BATCH = 1
SEQ = 16384
DEPTH = 1

N_HEADS = 16
Q_LORA_RANK = 512
KV_LORA_RANK = 512
QK_NOPE_DIM = 128
QK_ROPE_DIM = 64
V_HEAD_DIM = 128
ROPE_THETA = 10000.0
ATTN_WIDTH = N_HEADS * V_HEAD_DIM
Q_BLOCK = 128
POOL_WIDTH = D_MODEL // 2
POOL_WINDOWS = (2, 4, 8, 16)
POOL_GROUPS = len(POOL_WINDOWS)
POOL_GROUP_DIM = POOL_WIDTH // POOL_GROUPS
N_MEM = 256
MEM_HEADS = 4
MEM_HEAD_DIM = D_MODEL // MEM_HEADS
D_FF = 5632
CONV_WIDTH = 3
EPS = 1e-6
IN_SIZES = (Q_LORA_RANK, KV_LORA_RANK, QK_ROPE_DIM, POOL_WIDTH, D_MODEL, D_MODEL)
D_IN = Q_LORA_RANK + KV_LORA_RANK + QK_ROPE_DIM + POOL_WIDTH + 2 * D_MODEL

kernel_name = "hybrid_mla_pool_memxattn_convffn"


def rmsnorm(x, g):
    xf = x.astype(jnp.float32)
    y = xf * lax.rsqrt(jnp.mean(xf * xf, axis=-1, keepdims=True) + EPS)
    return (y * g.astype(jnp.float32)).astype(x.dtype)


def rope_tables(positions, dtype):
    inv_freq = ROPE_THETA ** (-jnp.arange(0, QK_ROPE_DIM, 2, dtype=jnp.float32) / QK_ROPE_DIM)
    ang = positions.astype(jnp.float32)[..., None] * inv_freq
    return jnp.cos(ang).astype(dtype), jnp.sin(ang).astype(dtype)


def apply_rope(x, cos, sin):
    half = QK_ROPE_DIM // 2
    x1, x2 = x[..., :half], x[..., half:]
    return jnp.concatenate([x1 * cos - x2 * sin, x2 * cos + x1 * sin], axis=-1)


def split_cols(z, sizes):
    outs, start = [], 0
    for n in sizes:
        outs.append(z[..., start:start + n])
        start += n
    return outs


def mla_branch(q_lat, c_kv, k_pe, positions, g_q_a, w_q_b, g_kv_a, w_kv_b):
    B, S, _ = q_lat.shape
    q = (rmsnorm(q_lat, g_q_a) @ w_q_b).reshape(B, S, N_HEADS, QK_NOPE_DIM + QK_ROPE_DIM)
    q_nope, q_pe = q[..., :QK_NOPE_DIM], q[..., QK_NOPE_DIM:]
    kv = (rmsnorm(c_kv, g_kv_a) @ w_kv_b).reshape(B, S, N_HEADS, QK_NOPE_DIM + V_HEAD_DIM)
    k_nope, v = kv[..., :QK_NOPE_DIM], kv[..., QK_NOPE_DIM:]
    cos, sin = rope_tables(positions, q.dtype)
    q_pe = apply_rope(q_pe, cos[:, :, None, :], sin[:, :, None, :])
    k_pe = apply_rope(k_pe, cos, sin)
    scale = (QK_NOPE_DIM + QK_ROPE_DIM) ** -0.5
    nb = S // Q_BLOCK
    qn_b = q_nope.reshape(B, nb, Q_BLOCK, N_HEADS, QK_NOPE_DIM).transpose(1, 0, 3, 2, 4)
    qp_b = q_pe.reshape(B, nb, Q_BLOCK, N_HEADS, QK_ROPE_DIM).transpose(1, 0, 3, 2, 4)
    kn = k_nope.transpose(0, 2, 1, 3)
    vh = v.transpose(0, 2, 1, 3)
    key_idx = jnp.arange(S)

    def query_block(args):
        qn, qp, i = args
        s = (jnp.einsum('bhqd,bhkd->bhqk', qn, kn)
             + jnp.einsum('bhqr,bkr->bhqk', qp, k_pe)).astype(jnp.float32) * scale
        q_idx = i * Q_BLOCK + jnp.arange(Q_BLOCK)
        s = jnp.where(key_idx[None, :] <= q_idx[:, None], s, -jnp.inf)
        p = jax.nn.softmax(s, axis=-1).astype(vh.dtype)
        return jnp.einsum('bhqk,bhkd->bhqd', p, vh)

    o = lax.map(query_block, (qn_b, qp_b, jnp.arange(nb)))
    return o.transpose(1, 0, 3, 2, 4).reshape(B, S, ATTN_WIDTH)


def pool_branch(u, w_pool_grp, pool_scale):
    B, S, _ = u.shape
    ug = u.reshape(B, S, POOL_GROUPS, POOL_GROUP_DIM).astype(jnp.float32)
    cs = jnp.concatenate([jnp.zeros((B, 1, POOL_GROUPS, POOL_GROUP_DIM), jnp.float32),
                          jnp.cumsum(ug, axis=1)], axis=1)
    t = jnp.arange(S)
    outs = []
    for g, w in enumerate(POOL_WINDOWS):
        lo = jnp.maximum(t + 1 - w, 0)
        win_sum = cs[:, 1:, g] - cs[:, lo, g]
        cnt = jnp.minimum(t + 1, w).astype(jnp.float32)[None, :, None]
        outs.append(win_sum / cnt - ug[:, :, g])
    d = jnp.stack(outs, axis=2).astype(u.dtype)
    y = jnp.einsum('bsgc,gcd->bsgd', d, w_pool_grp).reshape(B, S, POOL_WIDTH)
    return y * pool_scale


def memory_xattn(h, mem_n, w_mem_q, w_mem_kv, w_mem_o):
    B, S, _ = h.shape
    q = (h @ w_mem_q).reshape(B, S, MEM_HEADS, MEM_HEAD_DIM)
    kv = (mem_n @ w_mem_kv).reshape(B, N_MEM, 2, MEM_HEADS, MEM_HEAD_DIM)
    k, v = kv[:, :, 0], kv[:, :, 1]
    s = jnp.einsum('bqhd,bkhd->bhqk', q, k).astype(jnp.float32) * (MEM_HEAD_DIM ** -0.5)
    p = jax.nn.softmax(s, axis=-1).astype(v.dtype)
    o = jnp.einsum('bhqk,bkhd->bqhd', p, v).reshape(B, S, D_MODEL)
    return o @ w_mem_o


def conv_gated_ffn(h, w_ffn_in, conv_w, conv_b, w_ffn_out):
    S = h.shape[1]
    a = h @ w_ffn_in
    ap = jnp.pad(a, ((0, 0), (CONV_WIDTH - 1, 0), (0, 0)))
    c = conv_b
    for k in range(CONV_WIDTH):
        c = c + ap[:, k:k + S] * conv_w[k]
    gate, up = c[..., :D_FF], c[..., D_FF:]
    return (jax.nn.gelu(gate, approximate=True) * up) @ w_ffn_out


def setup_inputs(seed: int = 0) -> dict:
    key = jax.random.key(seed)
    ks = iter(jax.random.split(key, 40))
    L = DEPTH

    def w(shape, fan_in):
        return jax.random.normal(next(ks), shape, jnp.float32) * fan_in ** -0.5

    def gain(shape):
        return 1.0 + 0.05 * jax.random.normal(next(ks), shape, jnp.float32)

    x = jax.random.normal(next(ks), (BATCH, SEQ, D_MODEL), jnp.float32)
    mem = jax.random.normal(next(ks), (BATCH, N_MEM, D_MODEL), jnp.float32)
    offset = jax.random.randint(next(ks), (BATCH, 1), 0, 4096, jnp.int32)
    positions = (offset + jnp.arange(SEQ, dtype=jnp.int32)[None, :]).astype(jnp.int32)
    return {
        "x": x,
        "mem": mem,
        "positions": positions,
        "g_mix_pre": gain((L, D_MODEL)),
        "w_in": w((L, D_MODEL, D_IN), D_MODEL),
        "g_q_a": gain((L, Q_LORA_RANK)),
        "w_q_b": w((L, Q_LORA_RANK, N_HEADS * (QK_NOPE_DIM + QK_ROPE_DIM)), Q_LORA_RANK),
        "g_kv_a": gain((L, KV_LORA_RANK)),
        "w_kv_b": w((L, KV_LORA_RANK, N_HEADS * (QK_NOPE_DIM + V_HEAD_DIM)), KV_LORA_RANK),
        "w_attn_br": w((L, ATTN_WIDTH, D_MODEL), ATTN_WIDTH),
        "w_pool_grp": w((L, POOL_GROUPS, POOL_GROUP_DIM, POOL_GROUP_DIM), POOL_GROUP_DIM),
        "pool_scale": gain((L, POOL_WIDTH)),
        "w_pool_br": w((L, POOL_WIDTH, D_MODEL), POOL_WIDTH),
        "w_mix_out": w((L, D_MODEL, D_MODEL), D_MODEL),
        "g_mix_post": gain((L, D_MODEL)),
        "g_mem_pre": gain((L, D_MODEL)),
        "g_mem_kv": gain((L, D_MODEL)),
        "w_mem_q": w((L, D_MODEL, D_MODEL), D_MODEL),
        "w_mem_kv": w((L, D_MODEL, 2 * D_MODEL), D_MODEL),
        "w_mem_o": w((L, D_MODEL, D_MODEL), D_MODEL),
        "g_mem_post": gain((L, D_MODEL)),
        "g_ffn_pre": gain((L, D_MODEL)),
        "w_ffn_in": w((L, D_MODEL, 2 * D_FF), D_MODEL),
        "conv_w": w((L, CONV_WIDTH, 2 * D_FF), CONV_WIDTH),
        "conv_b": 0.01 * jax.random.normal(next(ks), (L, 2 * D_FF), jnp.float32),
        "w_ffn_out": w((L, D_FF, D_MODEL), D_FF),
        "g_ffn_post": gain((L, D_MODEL)),
    }


def reference(x, mem, positions, g_mix_pre, w_in, g_q_a, w_q_b, g_kv_a, w_kv_b,
              w_attn_br, w_pool_grp, pool_scale, w_pool_br, w_mix_out, g_mix_post,
              g_mem_pre, g_mem_kv, w_mem_q, w_mem_kv, w_mem_o, g_mem_post,
              g_ffn_pre, w_ffn_in, conv_w, conv_b, w_ffn_out, g_ffn_post):
    for l in range(DEPTH):
        h = rmsnorm(x, g_mix_pre[l])
        z = h @ w_in[l]
        q_lat, c_kv, k_pe, u_pool, gate_a, gate_p = split_cols(z, IN_SIZES)
        y_a = mla_branch(q_lat, c_kv, k_pe, positions, g_q_a[l], w_q_b[l],
                         g_kv_a[l], w_kv_b[l]) @ w_attn_br[l]
        y_p = pool_branch(u_pool, w_pool_grp[l], pool_scale[l]) @ w_pool_br[l]
        merged = jax.nn.sigmoid(gate_a) * y_a + jax.nn.sigmoid(gate_p) * y_p
        x = x + rmsnorm(merged @ w_mix_out[l], g_mix_post[l])
        h = rmsnorm(x, g_mem_pre[l])
        mem_n = rmsnorm(mem, g_mem_kv[l])
        x = x + rmsnorm(memory_xattn(h, mem_n, w_mem_q[l], w_mem_kv[l], w_mem_o[l]),
                        g_mem_post[l])
        h = rmsnorm(x, g_ffn_pre[l])
        x = x + rmsnorm(conv_gated_ffn(h, w_ffn_in[l], conv_w[l], conv_b[l], w_ffn_out[l]),
                        g_ffn_post[l])
    return x
```

```python
import functools
import math

import numpy as np
import jax
import jax.numpy as jnp
from jax import lax
from jax.experimental import pallas as pl
from jax.experimental.pallas import tpu as pltpu

F32 = jnp.float32
BF16 = jnp.bfloat16

D_MODEL = 2048
SEQ = 16384
N_HEADS = 16
Q_LORA = 512
KV_LORA = 512
NOPE = 128
ROPE = 64
V_DIM = 128
ROPE_THETA = 10000.0
ATTN_WIDTH = N_HEADS * V_DIM
POOL_WIDTH = 1024
POOL_WINDOWS = (2, 4, 8, 16)
POOL_GROUP_DIM = 256
N_MEM = 256
MEM_HEADS = 4
MEM_HEAD_DIM = 512
D_FF = 5632
EPS = 1e-6

Z_GATE_A = 0
Z_GATE_P = 2048
Z_POOL = 4096
Z_QLAT = 5120
Z_CKV = 5632
Z_KPE = 6144
Z_WIDTH = 6400

HEAD_PAD = 256
HALO = 16
MIB = 1024 * 1024


def _cparams(sem, vmem_mib):
    return pltpu.CompilerParams(dimension_semantics=sem,
                                vmem_limit_bytes=int(vmem_mib * MIB))


def _rms(xf, g):
    ms = jnp.mean(xf * xf, axis=-1, keepdims=True)
    return xf * lax.rsqrt(ms + EPS) * g


def _norm_rows(dst_ref, dst_off, src_ref, g, rows, chunk):
    def body(c, carry):
        r = pl.multiple_of(c * chunk, chunk)
        y = _rms(src_ref[pl.ds(r, chunk), :].astype(F32), g)
        dst_ref[pl.ds(dst_off + r, chunk), :] = y.astype(dst_ref.dtype)
        return carry
    lax.fori_loop(0, rows // chunk, body, 0)


def _in_proj_kernel(x_ref, g_ref, w_ref, z_ref, h_ref, *, tm):
    @pl.when(pl.program_id(1) == 0)
    def _():
        _norm_rows(h_ref, 0, x_ref, g_ref[...], tm, 128)
    z_ref[...] = jnp.dot(h_ref[...], w_ref[...],
                         preferred_element_type=F32).astype(z_ref.dtype)


def _in_proj(x, g, w, *, tm=1024, tn=1280):
    s, d = x.shape
    n = w.shape[1]
    return pl.pallas_call(
        functools.partial(_in_proj_kernel, tm=tm),
        grid=(s // tm, n // tn),
        in_specs=[pl.BlockSpec((tm, d), lambda i, j: (i, 0)),
                  pl.BlockSpec((1, d), lambda i, j: (0, 0)),
                  pl.BlockSpec((d, tn), lambda i, j: (0, j))],
        out_specs=pl.BlockSpec((tm, tn), lambda i, j: (i, j)),
        out_shape=jax.ShapeDtypeStruct((s, n), BF16),
        scratch_shapes=[pltpu.VMEM((tm, d), BF16)],
        compiler_params=_cparams(("parallel", "arbitrary"), 48),
        name="in_proj",
    )(x, g, w)


def _mla_proj_kernel(ql_ref, ckv_ref, kp_ref, pos_ref, freq_ref, gq_ref, gkv_ref,
                     wq_ref, wkv_ref, q_ref, k_ref, v_ref, *, q_scale):
    qn = _rms(ql_ref[...].astype(F32), gq_ref[...]).astype(BF16)
    cn = _rms(ckv_ref[...].astype(F32), gkv_ref[...]).astype(BF16)
    tm = qn.shape[0]
    ang = pos_ref[...] * freq_ref[...]
    lane = lax.broadcasted_iota(jnp.int32, (1, 2 * ROPE), 1)
    sign = jnp.where((lane >= ROPE) & (lane < ROPE + ROPE // 2), -1.0, 1.0)
    table = jnp.where(lane < ROPE, jnp.cos(ang), jnp.sin(ang) * sign)
    low = lane < ROPE

    def rot(t):
        t = t * table
        return t + pltpu.roll(t, ROPE, axis=1)

    k_pe = jnp.where(low, rot(kp_ref[...].astype(F32)), 0.0).astype(BF16)
    ones = jnp.ones((tm, HEAD_PAD - V_DIM), BF16)
    for h in range(N_HEADS):
        qh = jnp.dot(qn, wq_ref[h], preferred_element_type=F32)
        q_ref[h, :, 0:NOPE] = (qh[:, 0:NOPE] * q_scale).astype(BF16)
        q_ref[h, :, NOPE:HEAD_PAD] = (rot(qh[:, NOPE:HEAD_PAD]) * q_scale).astype(BF16)
        kvh = jnp.dot(cn, wkv_ref[h], preferred_element_type=F32)
        k_ref[h, :, 0:NOPE] = kvh[:, 0:NOPE].astype(BF16)
        k_ref[h, :, NOPE:HEAD_PAD] = k_pe
        v_ref[h, :, 0:V_DIM] = kvh[:, NOPE:HEAD_PAD].astype(BF16)
        v_ref[h, :, V_DIM:HEAD_PAD] = ones


def _mla_proj(z, pos, freq, gq, gkv, wq, wkv, *, tm=512):
    s = z.shape[0]
    q_scale = (NOPE + ROPE) ** -0.5 * math.log2(math.e)
    head_out = jax.ShapeDtypeStruct((N_HEADS, s, HEAD_PAD), BF16)
    head_spec = pl.BlockSpec((N_HEADS, tm, HEAD_PAD), lambda i: (0, i, 0))
    const3 = lambda i: (0, 0, 0)
    return pl.pallas_call(
        functools.partial(_mla_proj_kernel, q_scale=q_scale),
        grid=(s // tm,),
        in_specs=[pl.BlockSpec((tm, Q_LORA), lambda i: (i, Z_QLAT // Q_LORA)),
                  pl.BlockSpec((tm, KV_LORA), lambda i: (i, Z_CKV // KV_LORA)),
                  pl.BlockSpec((tm, 2 * ROPE), lambda i: (i, Z_KPE // (2 * ROPE))),
                  pl.BlockSpec((tm, 1), lambda i: (i, 0)),
                  pl.BlockSpec((1, 2 * ROPE), lambda i: (0, 0)),
                  pl.BlockSpec((1, Q_LORA), lambda i: (0, 0)),
                  pl.BlockSpec((1, KV_LORA), lambda i: (0, 0)),
                  pl.BlockSpec((N_HEADS, Q_LORA, HEAD_PAD), const3),
                  pl.BlockSpec((N_HEADS, KV_LORA, HEAD_PAD), const3)],
        out_specs=[head_spec, head_spec, head_spec],
        out_shape=[head_out, head_out, head_out],
        compiler_params=_cparams(("parallel",), 48),
        name="mla_proj",
    )(z, z, z, pos, freq, gq, gkv, wq, wkv)


def _flash_kernel(q_ref, k_ref, v_ref, o_ref, m_ref, acc_ref, *, tq, tk):
    i = pl.program_id(1)
    q = q_ref[0]
    m_ref[...] = jnp.full(m_ref.shape, -jnp.inf, F32)
    acc_ref[...] = jnp.zeros(acc_ref.shape, F32)

    def step(j, masked):
        c = pl.multiple_of(j * tk, tk)
        kt = k_ref[0, pl.ds(c, tk), :]
        s = lax.dot_general(q, kt, (((1,), (1,)), ((), ())),
                            preferred_element_type=F32)
        if masked:
            row = i * tq + lax.broadcasted_iota(jnp.int32, (tq, tk), 0)
            col = c + lax.broadcasted_iota(jnp.int32, (tq, tk), 1)
            s = jnp.where(col <= row, s, -jnp.inf)
        m_prev = m_ref[...]
        m_new = jnp.maximum(m_prev, jnp.max(s, axis=-1, keepdims=True))
        alpha = jnp.exp2(m_prev - m_new)
        p = jnp.exp2(s - m_new).astype(BF16)
        pv = jnp.dot(p, v_ref[0, pl.ds(c, tk), :], preferred_element_type=F32)
        acc_ref[...] = alpha * acc_ref[...] + pv
        m_ref[...] = m_new

    n_full = (i * tq) // tk

    def body(j, carry):
        step(j, False)
        return carry
    lax.fori_loop(0, n_full, body, 0)
    for d in range(tq // tk):
        step(n_full + d, True)
    acc = acc_ref[...]
    o_ref[...] = (acc[:, 0:V_DIM] / acc[:, V_DIM:HEAD_PAD]).astype(o_ref.dtype)


def _flash(q, k, v, *, tq=512, tk=512):
    h, s, _ = q.shape
    return pl.pallas_call(
        functools.partial(_flash_kernel, tq=tq, tk=tk),
        grid=(h, s // tq),
        in_specs=[pl.BlockSpec((1, tq, HEAD_PAD), lambda hh, i: (hh, i, 0)),
                  pl.BlockSpec((1, s, HEAD_PAD), lambda hh, i: (hh, 0, 0)),
                  pl.BlockSpec((1, s, HEAD_PAD), lambda hh, i: (hh, 0, 0))],
        out_specs=pl.BlockSpec((tq, V_DIM), lambda hh, i: (i, hh)),
        out_shape=jax.ShapeDtypeStruct((s, h * V_DIM), BF16),
        scratch_shapes=[pltpu.VMEM((tq, 1), F32),
                        pltpu.VMEM((tq, HEAD_PAD), F32)],
        compiler_params=_cparams(("parallel", "arbitrary"), 48),
        name="flash",
    )(q, k, v)


def _mix_kernel(o_ref, ga_ref, gp_ref, u_ref, uh_ref, x_ref, wab_ref, wpg_ref, ps_ref,
                wpb_ref, wmo_ref, g_ref, out_ref, ext_ref, yp_ref, *, tm):
    i = pl.program_id(0)
    ya = jnp.dot(o_ref[...], wab_ref[...], preferred_element_type=F32)
    ext_ref[0:HALO, :] = jnp.where(i == 0, 0.0, uh_ref[...].astype(F32))
    ext_ref[HALO:HALO + tm, :] = u_ref[...].astype(F32)
    t = i * tm + lax.broadcasted_iota(jnp.int32, (tm, 1), 0)
    for g, w in enumerate(POOL_WINDOWS):
        cols = slice(g * POOL_GROUP_DIM, (g + 1) * POOL_GROUP_DIM)
        cur = ext_ref[HALO:HALO + tm, cols]
        win = cur
        for b in range(1, w):
            win = win + ext_ref[HALO - b:HALO - b + tm, cols]
        cnt = jnp.minimum(t + 1, w).astype(F32)
        d = win / cnt - cur
        yg = jnp.dot(d.astype(BF16), wpg_ref[g], preferred_element_type=F32)
        yp_ref[:, cols] = (yg * ps_ref[:, cols]).astype(BF16)
    yp = jnp.dot(yp_ref[...], wpb_ref[...], preferred_element_type=F32)
    merged = (jax.nn.sigmoid(ga_ref[...].astype(F32)) * ya
              + jax.nn.sigmoid(gp_ref[...].astype(F32)) * yp)
    mo = jnp.dot(merged.astype(BF16), wmo_ref[...], preferred_element_type=F32)
    out_ref[...] = x_ref[...] + _rms(mo, g_ref[...])


def _resident(shape):
    zeros = (0,) * len(shape)
    return pl.BlockSpec(shape, lambda i: zeros, pipeline_mode=pl.Buffered(1))


def _mix(o, z, x, wab, wpg, ps, wpb, wmo, g, *, tm=256):
    s, d = x.shape
    return pl.pallas_call(
        functools.partial(_mix_kernel, tm=tm),
        grid=(s // tm,),
        in_specs=[pl.BlockSpec((tm, ATTN_WIDTH), lambda i: (i, 0)),
                  pl.BlockSpec((tm, d), lambda i: (i, Z_GATE_A // d)),
                  pl.BlockSpec((tm, d), lambda i: (i, Z_GATE_P // d)),
                  pl.BlockSpec((tm, POOL_WIDTH), lambda i: (i, Z_POOL // POOL_WIDTH)),
                  pl.BlockSpec((HALO, POOL_WIDTH),
                               lambda i: (jnp.maximum(i * (tm // HALO) - 1, 0),
                                          Z_POOL // POOL_WIDTH)),
                  pl.BlockSpec((tm, d), lambda i: (i, 0)),
                  _resident(wab.shape), _resident(wpg.shape), _resident(ps.shape),
                  _resident(wpb.shape), _resident(wmo.shape), _resident(g.shape)],
        out_specs=pl.BlockSpec((tm, d), lambda i: (i, 0)),
        out_shape=jax.ShapeDtypeStruct((s, d), F32),
        scratch_shapes=[pltpu.VMEM((HALO + tm, POOL_WIDTH), F32),
                        pltpu.VMEM((tm, POOL_WIDTH), BF16)],
        compiler_params=_cparams(("parallel",), 56),
        name="mix",
    )(o, z, z, z, z, x, wab, wpg, ps, wpb, wmo, g)


def _mem_kv_kernel(mem_ref, g_ref, w_ref, kv_ref):
    mn = _rms(mem_ref[...], g_ref[...]).astype(BF16)
    kv_ref[...] = jnp.dot(mn, w_ref[...], preferred_element_type=F32).astype(kv_ref.dtype)


def _mem_kv(mem, g, w, *, tn=1024):
    m, d = mem.shape
    n = w.shape[1]
    return pl.pallas_call(
        _mem_kv_kernel,
        grid=(n // tn,),
        in_specs=[pl.BlockSpec((m, d), lambda j: (0, 0)),
                  pl.BlockSpec((1, d), lambda j: (0, 0)),
                  pl.BlockSpec((d, tn), lambda j: (0, j))],
        out_specs=pl.BlockSpec((m, tn), lambda j: (0, j)),
        out_shape=jax.ShapeDtypeStruct((m, n), BF16),
        compiler_params=_cparams(("parallel",), 32),
        name="mem_kv",
    )(mem, g, w)


def _mem_attn_kernel(x_ref, gpre_ref, wq_ref, kv_ref, wo_ref, gpost_ref, out_ref,
                     o_ref):
    x = x_ref[...]
    h = _rms(x, gpre_ref[...]).astype(BF16)
    scale = MEM_HEAD_DIM ** -0.5
    q = (jnp.dot(h, wq_ref[...], preferred_element_type=F32) * scale).astype(BF16)
    for hd in range(MEM_HEADS):
        cols = slice(hd * MEM_HEAD_DIM, (hd + 1) * MEM_HEAD_DIM)
        vcols = slice(D_MODEL + hd * MEM_HEAD_DIM, D_MODEL + (hd + 1) * MEM_HEAD_DIM)
        s = lax.dot_general(q[:, cols], kv_ref[:, cols], (((1,), (1,)), ((), ())),
                            preferred_element_type=F32)
        p = jnp.exp(s - jnp.max(s, axis=-1, keepdims=True))
        l = jnp.sum(p, axis=-1, keepdims=True)
        oh = jnp.dot(p.astype(BF16), kv_ref[:, vcols], preferred_element_type=F32)
        o_ref[:, cols] = (oh / l).astype(BF16)
    y = jnp.dot(o_ref[...], wo_ref[...], preferred_element_type=F32)
    out_ref[...] = x + _rms(y, gpost_ref[...])


def _mem_attn(x, gpre, wq, kv, wo, gpost, *, tm=256):
    s, d = x.shape
    return pl.pallas_call(
        _mem_attn_kernel,
        grid=(s // tm,),
        in_specs=[pl.BlockSpec((tm, d), lambda i: (i, 0)),
                  _resident(gpre.shape), _resident(wq.shape), _resident(kv.shape),
                  _resident(wo.shape), _resident(gpost.shape)],
        out_specs=pl.BlockSpec((tm, d), lambda i: (i, 0)),
        out_shape=jax.ShapeDtypeStruct((s, d), F32),
        scratch_shapes=[pltpu.VMEM((tm, d), BF16)],
        compiler_params=_cparams(("parallel",), 48),
        name="mem_attn",
    )(x, gpre, wq, kv, wo, gpost)


def _gelu_tanh(x):
    c = math.sqrt(2.0 / math.pi)
    return 0.5 * x * (1.0 + jnp.tanh(c * (x + 0.044715 * (x * x * x))))


def _ffn_kernel(x_ref, xh_ref, gpre_ref, wg_ref, wu_ref, cwg_ref, cwu_ref, cbg_ref,
                cbu_ref, wo_ref, gpost_ref, out_ref, h_ref, a_ref, acc_ref, *, tm, tn):
    i = pl.program_id(0)
    j = pl.program_id(1)

    @pl.when(j == 0)
    def _():
        hh = _rms(xh_ref[...], gpre_ref[...])
        h_ref[0:HALO, :] = jnp.where(i == 0, 0.0, hh).astype(BF16)
        _norm_rows(h_ref, HALO, x_ref, gpre_ref[...], tm, 128)
        acc_ref[...] = jnp.zeros(acc_ref.shape, F32)

    def conv(w_ref, cw_ref, cb_ref):
        a_ref[...] = jnp.dot(h_ref[...], w_ref[...], preferred_element_type=F32)
        return (cb_ref[...]
                + a_ref[HALO - 2:HALO - 2 + tm, :] * cw_ref[0:1, :]
                + a_ref[HALO - 1:HALO - 1 + tm, :] * cw_ref[1:2, :]
                + a_ref[HALO:HALO + tm, :] * cw_ref[2:3, :])

    gate = conv(wg_ref, cwg_ref, cbg_ref)
    act = _gelu_tanh(gate)
    up = conv(wu_ref, cwu_ref, cbu_ref)
    acc_ref[...] += jnp.dot((act * up).astype(BF16), wo_ref[...],
                            preferred_element_type=F32)

    @pl.when(j == pl.num_programs(1) - 1)
    def _():
        out_ref[...] = x_ref[...] + _rms(acc_ref[...], gpost_ref[...])


def _ffn(x, gpre, w_in, conv_w, conv_b, w_out, gpost, *, tm=512, tn=512):
    s, d = x.shape
    nj = D_FF // tn
    return pl.pallas_call(
        functools.partial(_ffn_kernel, tm=tm, tn=tn),
        grid=(s // tm, nj),
        in_specs=[pl.BlockSpec((tm, d), lambda i, j: (i, 0)),
                  pl.BlockSpec((HALO, d),
                               lambda i, j: (jnp.maximum(i * (tm // HALO) - 1, 0), 0)),
                  pl.BlockSpec((1, d), lambda i, j: (0, 0)),
                  pl.BlockSpec((d, tn), lambda i, j: (0, j)),
                  pl.BlockSpec((d, tn), lambda i, j: (0, nj + j)),
                  pl.BlockSpec((3, tn), lambda i, j: (0, j)),
                  pl.BlockSpec((3, tn), lambda i, j: (0, nj + j)),
                  pl.BlockSpec((1, tn), lambda i, j: (0, j)),
                  pl.BlockSpec((1, tn), lambda i, j: (0, nj + j)),
                  pl.BlockSpec((tn, d), lambda i, j: (j, 0)),
                  pl.BlockSpec((1, d), lambda i, j: (0, 0))],
        out_specs=pl.BlockSpec((tm, d), lambda i, j: (i, 0)),
        out_shape=jax.ShapeDtypeStruct((s, d), F32),
        scratch_shapes=[pltpu.VMEM((HALO + tm, d), BF16),
                        pltpu.VMEM((HALO + tm, tn), F32),
                        pltpu.VMEM((tm, d), F32)],
        compiler_params=_cparams(("parallel", "arbitrary"), 48),
        name="ffn",
    )(x, x, gpre, w_in, w_in, conv_w, conv_w, conv_b, conv_b, w_out, gpost)


def _swap_halves(w):
    half = ROPE // 2
    return jnp.concatenate([w[..., half:], w[..., :half]], axis=-1)


def _prep_w_in(w_in):
    q_lat, c_kv, k_pe, u_pool, gate_a, gate_p = jnp.split(
        w_in, np.cumsum([Q_LORA, KV_LORA, ROPE, POOL_WIDTH, D_MODEL])[:].tolist(), axis=1)
    pad = jnp.zeros((D_MODEL, Z_WIDTH - Z_KPE - 2 * ROPE), w_in.dtype)
    return jnp.concatenate([gate_a, gate_p, u_pool, q_lat, c_kv, k_pe,
                            _swap_halves(k_pe), pad], axis=1).astype(BF16)


def _prep_w_q(w_q_b):
    w = w_q_b.reshape(Q_LORA, N_HEADS, NOPE + ROPE).transpose(1, 0, 2)
    pe = w[..., NOPE:]
    return jnp.concatenate([w, _swap_halves(pe)], axis=-1).astype(BF16)


def _prep_w_kv(w_kv_b):
    return w_kv_b.reshape(KV_LORA, N_HEADS, NOPE + V_DIM).transpose(1, 0, 2).astype(BF16)


def kernel(x, mem, positions, g_mix_pre, w_in, g_q_a, w_q_b, g_kv_a, w_kv_b, w_attn_br,
           w_pool_grp, pool_scale, w_pool_br, w_mix_out, g_mix_post, g_mem_pre, g_mem_kv,
           w_mem_q, w_mem_kv, w_mem_o, g_mem_post, g_ffn_pre, w_ffn_in, conv_w, conv_b,
           w_ffn_out, g_ffn_post):
    batch, seq, d = x.shape
    depth = w_in.shape[0]
    assert (batch, seq, d) == (1, SEQ, D_MODEL)
    inv_freq = ROPE_THETA ** (-np.arange(0, ROPE, 2, dtype=np.float32) / ROPE)
    freq = jnp.asarray(np.tile(inv_freq.astype(np.float32), 4)[None, :])
    pos = positions.reshape(seq, 1).astype(F32)
    xs = x.reshape(seq, d)
    mems = mem.reshape(N_MEM, d)
    for l in range(depth):
        z = _in_proj(xs, g_mix_pre[l][None], _prep_w_in(w_in[l]))
        q, k, v = _mla_proj(z, pos, freq, g_q_a[l][None], g_kv_a[l][None],
                            _prep_w_q(w_q_b[l]), _prep_w_kv(w_kv_b[l]))
        o = _flash(q, k, v)
        x1 = _mix(o, z, xs, w_attn_br[l].astype(BF16), w_pool_grp[l].astype(BF16),
                  pool_scale[l][None], w_pool_br[l].astype(BF16),
                  w_mix_out[l].astype(BF16), g_mix_post[l][None])
        kv = _mem_kv(mems, g_mem_kv[l][None], w_mem_kv[l].astype(BF16))
        x2 = _mem_attn(x1, g_mem_pre[l][None], w_mem_q[l].astype(BF16), kv,
                       w_mem_o[l].astype(BF16), g_mem_post[l][None])
        xs = _ffn(x2, g_ffn_pre[l][None], w_ffn_in[l].astype(BF16), conv_w[l],
                  conv_b[l][None], w_ffn_out[l].astype(BF16), g_ffn_post[l][None])
    return xs.reshape(batch, seq, d)
```

```python
import functools
import math

import numpy as np
import jax
import jax.numpy as jnp
from jax import lax
from jax.experimental import pallas as pl
from jax.experimental.pallas import tpu as pltpu

F32 = jnp.float32
BF16 = jnp.bfloat16

D_MODEL = 2048
SEQ = 16384
N_HEADS = 16
Q_LORA = 512
KV_LORA = 512
NOPE = 128
ROPE = 64
V_DIM = 128
ROPE_THETA = 10000.0
ATTN_WIDTH = N_HEADS * V_DIM
POOL_WIDTH = 1024
POOL_WINDOWS = (2, 4, 8, 16)
POOL_GROUP_DIM = 256
N_MEM = 256
MEM_HEADS = 4
MEM_HEAD_DIM = 512
D_FF = 5632
EPS = 1e-6

Z_GATE_A = 0
Z_GATE_P = 2048
Z_POOL = 4096
Z_QLAT = 5120
Z_CKV = 5632
Z_KPE = 6144
Z_WIDTH = 6400

HEAD_PAD = 256
HALO = 16
MIB = 1024 * 1024


def _cparams(sem, vmem_mib):
    return pltpu.CompilerParams(dimension_semantics=sem,
                                vmem_limit_bytes=int(vmem_mib * MIB))


def _rms(xf, g):
    ms = jnp.mean(xf * xf, axis=-1, keepdims=True)
    return xf * lax.rsqrt(ms + EPS) * g


def _norm_rows(dst_ref, dst_off, src_ref, g, rows, chunk):
    def body(c, carry):
        r = pl.multiple_of(c * chunk, chunk)
        y = _rms(src_ref[pl.ds(r, chunk), :].astype(F32), g)
        dst_ref[pl.ds(dst_off + r, chunk), :] = y.astype(dst_ref.dtype)
        return carry
    lax.fori_loop(0, rows // chunk, body, 0)


def _in_proj_kernel(x_ref, g_ref, w_ref, z_ref, h_ref, *, tm):
    @pl.when(pl.program_id(1) == 0)
    def _():
        _norm_rows(h_ref, 0, x_ref, g_ref[...], tm, 128)
    z_ref[...] = jnp.dot(h_ref[...], w_ref[...],
                         preferred_element_type=F32).astype(z_ref.dtype)


def _in_proj(x, g, w, *, tm=1024, tn=1280):
    s, d = x.shape
    n = w.shape[1]
    return pl.pallas_call(
        functools.partial(_in_proj_kernel, tm=tm),
        grid=(s // tm, n // tn),
        in_specs=[pl.BlockSpec((tm, d), lambda i, j: (i, 0)),
                  pl.BlockSpec((1, d), lambda i, j: (0, 0)),
                  pl.BlockSpec((d, tn), lambda i, j: (0, j))],
        out_specs=pl.BlockSpec((tm, tn), lambda i, j: (i, j)),
        out_shape=jax.ShapeDtypeStruct((s, n), BF16),
        scratch_shapes=[pltpu.VMEM((tm, d), BF16)],
        compiler_params=_cparams(("parallel", "arbitrary"), 48),
        name="in_proj",
    )(x, g, w)


def _mla_proj_kernel(ql_ref, ckv_ref, kp_ref, pos_ref, freq_ref, gq_ref, gkv_ref,
                     wq_ref, wkv_ref, q_ref, k_ref, v_ref, *, q_scale):
    qn = _rms(ql_ref[...].astype(F32), gq_ref[...]).astype(BF16)
    cn = _rms(ckv_ref[...].astype(F32), gkv_ref[...]).astype(BF16)
    tm = qn.shape[0]
    ang = pos_ref[...] * freq_ref[...]
    lane = lax.broadcasted_iota(jnp.int32, (1, 2 * ROPE), 1)
    sign = jnp.where((lane >= ROPE) & (lane < ROPE + ROPE // 2), -1.0, 1.0)
    table = jnp.where(lane < ROPE, jnp.cos(ang), jnp.sin(ang) * sign)
    low = lane < ROPE

    def rot(t):
        t = t * table
        return t + pltpu.roll(t, ROPE, axis=1)

    k_pe = jnp.where(low, rot(kp_ref[...].astype(F32)), 0.0).astype(BF16)
    ones = jnp.ones((tm, HEAD_PAD - V_DIM), BF16)
    for h in range(N_HEADS):
        qh = jnp.dot(qn, wq_ref[h], preferred_element_type=F32)
        q_ref[h, :, 0:NOPE] = (qh[:, 0:NOPE] * q_scale).astype(BF16)
        q_ref[h, :, NOPE:HEAD_PAD] = (rot(qh[:, NOPE:HEAD_PAD]) * q_scale).astype(BF16)
        kvh = jnp.dot(cn, wkv_ref[h], preferred_element_type=F32)
        k_ref[h, :, 0:NOPE] = kvh[:, 0:NOPE].astype(BF16)
        k_ref[h, :, NOPE:HEAD_PAD] = k_pe
        v_ref[h, :, 0:V_DIM] = kvh[:, NOPE:HEAD_PAD].astype(BF16)
        v_ref[h, :, V_DIM:HEAD_PAD] = ones


def _mla_proj(z, pos, freq, gq, gkv, wq, wkv, *, tm=512):
    s = z.shape[0]
    q_scale = (NOPE + ROPE) ** -0.5 * math.log2(math.e)
    head_out = jax.ShapeDtypeStruct((N_HEADS, s, HEAD_PAD), BF16)
    head_spec = pl.BlockSpec((N_HEADS, tm, HEAD_PAD), lambda i: (0, i, 0))
    const3 = lambda i: (0, 0, 0)
    return pl.pallas_call(
        functools.partial(_mla_proj_kernel, q_scale=q_scale),
        grid=(s // tm,),
        in_specs=[pl.BlockSpec((tm, Q_LORA), lambda i: (i, Z_QLAT // Q_LORA)),
                  pl.BlockSpec((tm, KV_LORA), lambda i: (i, Z_CKV // KV_LORA)),
                  pl.BlockSpec((tm, 2 * ROPE), lambda i: (i, Z_KPE // (2 * ROPE))),
                  pl.BlockSpec((tm, 1), lambda i: (i, 0)),
                  pl.BlockSpec((1, 2 * ROPE), lambda i: (0, 0)),
                  pl.BlockSpec((1, Q_LORA), lambda i: (0, 0)),
                  pl.BlockSpec((1, KV_LORA), lambda i: (0, 0)),
                  pl.BlockSpec((N_HEADS, Q_LORA, HEAD_PAD), const3),
                  pl.BlockSpec((N_HEADS, KV_LORA, HEAD_PAD), const3)],
        out_specs=[head_spec, head_spec, head_spec],
        out_shape=[head_out, head_out, head_out],
        compiler_params=_cparams(("parallel",), 48),
        name="mla_proj",
    )(z, z, z, pos, freq, gq, gkv, wq, wkv)


M_INIT = float(np.finfo(np.float32).min)


def _flash_kernel(q_ref, k_ref, v_ref, o_ref, s0, s1, p0, p1, a0, a1, m_ref, acc_ref,
                  *, tq, tk, rb, group):
    r = tk // tq
    nb = rb // r
    band = pl.program_id(1)
    row0 = band * rb
    row_end = row0 + rb
    n_plain = r * nb * nb * band + r * (nb * (nb - 1) // 2)
    for ref in (s0, s1, p0, p1, acc_ref):
        ref[...] = jnp.zeros(ref.shape, ref.dtype)
    a0[...] = jnp.ones(a0.shape, F32)
    a1[...] = jnp.ones(a1.shape, F32)
    m_ref[...] = jnp.full(m_ref.shape, M_INIT, F32)
    slots = ((s0, p0, a0), (s1, p1, a1))

    def last_tile(i):
        return lax.div(i, jnp.int32(r))

    def stage_a(i, j, slot):
        q = q_ref[0, pl.ds(pl.multiple_of((i - row0) * tq, tq), tq), :]
        kt = k_ref[0, pl.ds(pl.multiple_of(j * tk, tk), tk), :]
        slots[slot][0][...] = lax.dot_general(q, kt, (((1,), (1,)), ((), ())),
                                              preferred_element_type=F32)

    def stage_b(row, src, dst, off=None):
        s = slots[src][0][...]
        if off is not None:
            rel = (lax.broadcasted_iota(jnp.int32, (tq, tk), 1)
                   - lax.broadcasted_iota(jnp.int32, (tq, tk), 0))
            s = jnp.where(rel <= off, s, -jnp.inf)
        m_prev = m_ref[row]
        m_new = jnp.maximum(m_prev, jnp.max(s, axis=-1, keepdims=True))
        slots[dst][2][...] = jnp.exp2(m_prev - m_new)
        slots[dst][1][...] = jnp.exp2(s - m_new).astype(BF16)
        m_ref[row] = m_new

    def stage_c(row, j, src):
        vt = v_ref[0, pl.ds(pl.multiple_of(j * tk, tk), tk), :]
        acc_ref[row] = slots[src][2][...] * acc_ref[row] + jnp.dot(
            slots[src][1][...], vt, preferred_element_type=F32)

    def advance(i, j):
        same_row = j + 1 < last_tile(i)
        more_rows = i + 1 < row_end
        i2 = jnp.where(same_row, i, jnp.where(more_rows, i + 1, row0))
        j2 = jnp.where(same_row, j + 1, jnp.where(more_rows, 0, last_tile(row0)))
        return i2, j2

    first = jnp.maximum(row0, r)
    zero = jnp.int32(0)
    stage_a(first, zero, 0)

    def plain_group(_, state):
        ia, ja, ib, jb, ic, jc = state
        for g in range(group):
            cur, prev = (g + 1) % 2, g % 2
            stage_a(ia, ja, cur)
            stage_b(ib - row0, prev, cur)
            stage_c(ic - row0, jc, prev)
            ia, ja, ib, jb, ic, jc = advance(ia, ja) + (ia, ja, ib, jb)
        return ia, ja, ib, jb, ic, jc

    state = advance(first, zero) + (first, zero, first, zero)
    state = lax.fori_loop(0, n_plain // group, plain_group, state)

    _, _, _, _, ic0, jc0 = state

    def edge_tick(k, cur, prev):
        ka = jnp.minimum(k + 1, rb - 1)
        stage_a(row0 + ka, last_tile(row0 + ka), cur)
        kb = jnp.minimum(k, rb - 1)
        ib = row0 + kb
        off = jnp.where(k < rb, ib * tq - last_tile(ib) * tk, -tq)
        stage_b(kb, prev, cur, off)
        kc = jnp.clip(k - 1, 0, rb - 1)
        first_tick = k == 0
        row_c = jnp.where(first_tick, ic0 - row0, kc)
        stage_c(row_c, jnp.where(first_tick, jc0, last_tile(row0 + kc)), prev)

        @pl.when((k >= 1) & (k <= rb))
        def _():
            acc = acc_ref[kc]
            o_ref[pl.ds(pl.multiple_of(kc * tq, tq), tq), :] = (
                acc[:, 0:V_DIM] / acc[:, V_DIM:HEAD_PAD]).astype(o_ref.dtype)

    def edge_pair(kk, carry):
        edge_tick(2 * kk, 1, 0)
        edge_tick(2 * kk + 1, 0, 1)
        return carry
    lax.fori_loop(0, (rb + 2) // 2, edge_pair, 0)


def _flash(q, k, v, *, tq=512, tk=1024, rb=8, group=4):
    h, s, _ = q.shape
    band = tq * rb
    r = tk // tq
    nb = rb // r
    assert tk % tq == 0 and rb % r == 0 and s % band == 0
    assert group % 2 == 0 and nb >= 2
    assert (r * nb * nb) % group == 0 and (r * (nb * (nb - 1) // 2)) % group == 0
    return pl.pallas_call(
        functools.partial(_flash_kernel, tq=tq, tk=tk, rb=rb, group=group),
        grid=(h, s // band),
        in_specs=[pl.BlockSpec((1, band, HEAD_PAD), lambda hh, b: (hh, b, 0)),
                  pl.BlockSpec((1, s, HEAD_PAD), lambda hh, b: (hh, 0, 0)),
                  pl.BlockSpec((1, s, HEAD_PAD), lambda hh, b: (hh, 0, 0))],
        out_specs=pl.BlockSpec((band, V_DIM), lambda hh, b: (b, hh)),
        out_shape=jax.ShapeDtypeStruct((s, h * V_DIM), BF16),
        scratch_shapes=[pltpu.VMEM((tq, tk), F32), pltpu.VMEM((tq, tk), F32),
                        pltpu.VMEM((tq, tk), BF16), pltpu.VMEM((tq, tk), BF16),
                        pltpu.VMEM((tq, 1), F32), pltpu.VMEM((tq, 1), F32),
                        pltpu.VMEM((rb, tq, 1), F32),
                        pltpu.VMEM((rb, tq, HEAD_PAD), F32)],
        compiler_params=_cparams(("parallel", "arbitrary"), 58),
        name="flash",
    )(q, k, v)


def _mix_kernel(o_ref, ga_ref, gp_ref, u_ref, uh_ref, x_ref, wab_ref, wpg_ref, ps_ref,
                wpb_ref, wmo_ref, g_ref, out_ref, ext_ref, yp_ref, *, tm):
    i = pl.program_id(0)
    ya = jnp.dot(o_ref[...], wab_ref[...], preferred_element_type=F32)
    ext_ref[0:HALO, :] = jnp.where(i == 0, 0.0, uh_ref[...].astype(F32))
    ext_ref[HALO:HALO + tm, :] = u_ref[...].astype(F32)
    t = i * tm + lax.broadcasted_iota(jnp.int32, (tm, 1), 0)
    for g, w in enumerate(POOL_WINDOWS):
        cols = slice(g * POOL_GROUP_DIM, (g + 1) * POOL_GROUP_DIM)
        cur = ext_ref[HALO:HALO + tm, cols]
        win = cur
        for b in range(1, w):
            win = win + ext_ref[HALO - b:HALO - b + tm, cols]
        cnt = jnp.minimum(t + 1, w).astype(F32)
        d = win / cnt - cur
        yg = jnp.dot(d.astype(BF16), wpg_ref[g], preferred_element_type=F32)
        yp_ref[:, cols] = (yg * ps_ref[:, cols]).astype(BF16)
    yp = jnp.dot(yp_ref[...], wpb_ref[...], preferred_element_type=F32)
    merged = (jax.nn.sigmoid(ga_ref[...].astype(F32)) * ya
              + jax.nn.sigmoid(gp_ref[...].astype(F32)) * yp)
    mo = jnp.dot(merged.astype(BF16), wmo_ref[...], preferred_element_type=F32)
    out_ref[...] = x_ref[...] + _rms(mo, g_ref[...])


def _resident(shape):
    zeros = (0,) * len(shape)
    return pl.BlockSpec(shape, lambda i: zeros, pipeline_mode=pl.Buffered(1))


def _mix(o, z, x, wab, wpg, ps, wpb, wmo, g, *, tm=256):
    s, d = x.shape
    return pl.pallas_call(
        functools.partial(_mix_kernel, tm=tm),
        grid=(s // tm,),
        in_specs=[pl.BlockSpec((tm, ATTN_WIDTH), lambda i: (i, 0)),
                  pl.BlockSpec((tm, d), lambda i: (i, Z_GATE_A // d)),
                  pl.BlockSpec((tm, d), lambda i: (i, Z_GATE_P // d)),
                  pl.BlockSpec((tm, POOL_WIDTH), lambda i: (i, Z_POOL // POOL_WIDTH)),
                  pl.BlockSpec((HALO, POOL_WIDTH),
                               lambda i: (jnp.maximum(i * (tm // HALO) - 1, 0),
                                          Z_POOL // POOL_WIDTH)),
                  pl.BlockSpec((tm, d), lambda i: (i, 0)),
                  _resident(wab.shape), _resident(wpg.shape), _resident(ps.shape),
                  _resident(wpb.shape), _resident(wmo.shape), _resident(g.shape)],
        out_specs=pl.BlockSpec((tm, d), lambda i: (i, 0)),
        out_shape=jax.ShapeDtypeStruct((s, d), F32),
        scratch_shapes=[pltpu.VMEM((HALO + tm, POOL_WIDTH), F32),
                        pltpu.VMEM((tm, POOL_WIDTH), BF16)],
        compiler_params=_cparams(("parallel",), 56),
        name="mix",
    )(o, z, z, z, z, x, wab, wpg, ps, wpb, wmo, g)


def _mem_kv_kernel(mem_ref, g_ref, w_ref, kv_ref):
    mn = _rms(mem_ref[...], g_ref[...]).astype(BF16)
    kv_ref[...] = jnp.dot(mn, w_ref[...], preferred_element_type=F32).astype(kv_ref.dtype)


def _mem_kv(mem, g, w, *, tn=1024):
    m, d = mem.shape
    n = w.shape[1]
    return pl.pallas_call(
        _mem_kv_kernel,
        grid=(n // tn,),
        in_specs=[pl.BlockSpec((m, d), lambda j: (0, 0)),
                  pl.BlockSpec((1, d), lambda j: (0, 0)),
                  pl.BlockSpec((d, tn), lambda j: (0, j))],
        out_specs=pl.BlockSpec((m, tn), lambda j: (0, j)),
        out_shape=jax.ShapeDtypeStruct((m, n), BF16),
        compiler_params=_cparams(("parallel",), 32),
        name="mem_kv",
    )(mem, g, w)


def _mem_attn_kernel(x_ref, gpre_ref, wq_ref, kv_ref, wo_ref, gpost_ref, out_ref,
                     o_ref):
    x = x_ref[...]
    h = _rms(x, gpre_ref[...]).astype(BF16)
    scale = MEM_HEAD_DIM ** -0.5
    q = (jnp.dot(h, wq_ref[...], preferred_element_type=F32) * scale).astype(BF16)
    for hd in range(MEM_HEADS):
        cols = slice(hd * MEM_HEAD_DIM, (hd + 1) * MEM_HEAD_DIM)
        vcols = slice(D_MODEL + hd * MEM_HEAD_DIM, D_MODEL + (hd + 1) * MEM_HEAD_DIM)
        s = lax.dot_general(q[:, cols], kv_ref[:, cols], (((1,), (1,)), ((), ())),
                            preferred_element_type=F32)
        p = jnp.exp(s - jnp.max(s, axis=-1, keepdims=True))
        l = jnp.sum(p, axis=-1, keepdims=True)
        oh = jnp.dot(p.astype(BF16), kv_ref[:, vcols], preferred_element_type=F32)
        o_ref[:, cols] = (oh / l).astype(BF16)
    y = jnp.dot(o_ref[...], wo_ref[...], preferred_element_type=F32)
    out_ref[...] = x + _rms(y, gpost_ref[...])


def _mem_attn(x, gpre, wq, kv, wo, gpost, *, tm=256):
    s, d = x.shape
    return pl.pallas_call(
        _mem_attn_kernel,
        grid=(s // tm,),
        in_specs=[pl.BlockSpec((tm, d), lambda i: (i, 0)),
                  _resident(gpre.shape), _resident(wq.shape), _resident(kv.shape),
                  _resident(wo.shape), _resident(gpost.shape)],
        out_specs=pl.BlockSpec((tm, d), lambda i: (i, 0)),
        out_shape=jax.ShapeDtypeStruct((s, d), F32),
        scratch_shapes=[pltpu.VMEM((tm, d), BF16)],
        compiler_params=_cparams(("parallel",), 48),
        name="mem_attn",
    )(x, gpre, wq, kv, wo, gpost)


def _gelu_tanh(x):
    c = math.sqrt(2.0 / math.pi)
    return 0.5 * x * (1.0 + jnp.tanh(c * (x + 0.044715 * (x * x * x))))


def _ffn_kernel(x_ref, xh_ref, gpre_ref, wg_ref, wu_ref, cwg_ref, cwu_ref, cbg_ref,
                cbu_ref, wo_ref, gpost_ref, out_ref, h_ref, a_ref, acc_ref, *, tm, tn):
    i = pl.program_id(0)
    j = pl.program_id(1)

    @pl.when(j == 0)
    def _():
        hh = _rms(xh_ref[...], gpre_ref[...])
        h_ref[0:HALO, :] = jnp.where(i == 0, 0.0, hh).astype(BF16)
        _norm_rows(h_ref, HALO, x_ref, gpre_ref[...], tm, 128)
        acc_ref[...] = jnp.zeros(acc_ref.shape, F32)

    def conv(w_ref, cw_ref, cb_ref):
        a_ref[...] = jnp.dot(h_ref[...], w_ref[...], preferred_element_type=F32)
        return (cb_ref[...]
                + a_ref[HALO - 2:HALO - 2 + tm, :] * cw_ref[0:1, :]
                + a_ref[HALO - 1:HALO - 1 + tm, :] * cw_ref[1:2, :]
                + a_ref[HALO:HALO + tm, :] * cw_ref[2:3, :])

    gate = conv(wg_ref, cwg_ref, cbg_ref)
    act = _gelu_tanh(gate)
    up = conv(wu_ref, cwu_ref, cbu_ref)
    acc_ref[...] += jnp.dot((act * up).astype(BF16), wo_ref[...],
                            preferred_element_type=F32)

    @pl.when(j == pl.num_programs(1) - 1)
    def _():
        out_ref[...] = x_ref[...] + _rms(acc_ref[...], gpost_ref[...])


def _ffn(x, gpre, w_in, conv_w, conv_b, w_out, gpost, *, tm=512, tn=512):
    s, d = x.shape
    nj = D_FF // tn
    return pl.pallas_call(
        functools.partial(_ffn_kernel, tm=tm, tn=tn),
        grid=(s // tm, nj),
        in_specs=[pl.BlockSpec((tm, d), lambda i, j: (i, 0)),
                  pl.BlockSpec((HALO, d),
                               lambda i, j: (jnp.maximum(i * (tm // HALO) - 1, 0), 0)),
                  pl.BlockSpec((1, d), lambda i, j: (0, 0)),
                  pl.BlockSpec((d, tn), lambda i, j: (0, j)),
                  pl.BlockSpec((d, tn), lambda i, j: (0, nj + j)),
                  pl.BlockSpec((3, tn), lambda i, j: (0, j)),
                  pl.BlockSpec((3, tn), lambda i, j: (0, nj + j)),
                  pl.BlockSpec((1, tn), lambda i, j: (0, j)),
                  pl.BlockSpec((1, tn), lambda i, j: (0, nj + j)),
                  pl.BlockSpec((tn, d), lambda i, j: (j, 0)),
                  pl.BlockSpec((1, d), lambda i, j: (0, 0))],
        out_specs=pl.BlockSpec((tm, d), lambda i, j: (i, 0)),
        out_shape=jax.ShapeDtypeStruct((s, d), F32),
        scratch_shapes=[pltpu.VMEM((HALO + tm, d), BF16),
                        pltpu.VMEM((HALO + tm, tn), F32),
                        pltpu.VMEM((tm, d), F32)],
        compiler_params=_cparams(("parallel", "arbitrary"), 48),
        name="ffn",
    )(x, x, gpre, w_in, w_in, conv_w, conv_w, conv_b, conv_b, w_out, gpost)


def _swap_halves(w):
    half = ROPE // 2
    return jnp.concatenate([w[..., half:], w[..., :half]], axis=-1)


def _prep_w_in(w_in):
    q_lat, c_kv, k_pe, u_pool, gate_a, gate_p = jnp.split(
        w_in, np.cumsum([Q_LORA, KV_LORA, ROPE, POOL_WIDTH, D_MODEL])[:].tolist(), axis=1)
    pad = jnp.zeros((D_MODEL, Z_WIDTH - Z_KPE - 2 * ROPE), w_in.dtype)
    return jnp.concatenate([gate_a, gate_p, u_pool, q_lat, c_kv, k_pe,
                            _swap_halves(k_pe), pad], axis=1).astype(BF16)


def _prep_w_q(w_q_b):
    w = w_q_b.reshape(Q_LORA, N_HEADS, NOPE + ROPE).transpose(1, 0, 2)
    pe = w[..., NOPE:]
    return jnp.concatenate([w, _swap_halves(pe)], axis=-1).astype(BF16)


def _prep_w_kv(w_kv_b):
    return w_kv_b.reshape(KV_LORA, N_HEADS, NOPE + V_DIM).transpose(1, 0, 2).astype(BF16)


def kernel(x, mem, positions, g_mix_pre, w_in, g_q_a, w_q_b, g_kv_a, w_kv_b, w_attn_br,
           w_pool_grp, pool_scale, w_pool_br, w_mix_out, g_mix_post, g_mem_pre, g_mem_kv,
           w_mem_q, w_mem_kv, w_mem_o, g_mem_post, g_ffn_pre, w_ffn_in, conv_w, conv_b,
           w_ffn_out, g_ffn_post):
    batch, seq, d = x.shape
    depth = w_in.shape[0]
    assert (batch, seq, d) == (1, SEQ, D_MODEL)
    inv_freq = ROPE_THETA ** (-np.arange(0, ROPE, 2, dtype=np.float32) / ROPE)
    freq = jnp.asarray(np.tile(inv_freq.astype(np.float32), 4)[None, :])
    pos = positions.reshape(seq, 1).astype(F32)
    xs = x.reshape(seq, d)
    mems = mem.reshape(N_MEM, d)
    for l in range(depth):
        z = _in_proj(xs, g_mix_pre[l][None], _prep_w_in(w_in[l]))
        q, k, v = _mla_proj(z, pos, freq, g_q_a[l][None], g_kv_a[l][None],
                            _prep_w_q(w_q_b[l]), _prep_w_kv(w_kv_b[l]))
        o = _flash(q, k, v)
        x1 = _mix(o, z, xs, w_attn_br[l].astype(BF16), w_pool_grp[l].astype(BF16),
                  pool_scale[l][None], w_pool_br[l].astype(BF16),
                  w_mix_out[l].astype(BF16), g_mix_post[l][None])
        kv = _mem_kv(mems, g_mem_kv[l][None], w_mem_kv[l].astype(BF16))
        x2 = _mem_attn(x1, g_mem_pre[l][None], w_mem_q[l].astype(BF16), kv,
                       w_mem_o[l].astype(BF16), g_mem_post[l][None])
        xs = _ffn(x2, g_ffn_pre[l][None], w_ffn_in[l].astype(BF16), conv_w[l],
                  conv_b[l][None], w_ffn_out[l].astype(BF16), g_ffn_post[l][None])
    return xs.reshape(batch, seq, d)
```

```python
import functools
import math

import numpy as np
import jax
import jax.numpy as jnp
from jax import lax
from jax.experimental import pallas as pl
from jax.experimental.pallas import tpu as pltpu

F32 = jnp.float32
BF16 = jnp.bfloat16

D_MODEL = 2048
SEQ = 16384
N_HEADS = 16
Q_LORA = 512
KV_LORA = 512
NOPE = 128
ROPE = 64
V_DIM = 128
ROPE_THETA = 10000.0
ATTN_WIDTH = N_HEADS * V_DIM
POOL_WIDTH = 1024
POOL_WINDOWS = (2, 4, 8, 16)
POOL_GROUP_DIM = 256
N_MEM = 256
MEM_HEADS = 4
MEM_HEAD_DIM = 512
D_FF = 5632
EPS = 1e-6

Z_GATE_A = 0
Z_GATE_P = 2048
Z_POOL = 4096
Z_QLAT = 5120
Z_CKV = 5632
Z_KPE = 6144
Z_WIDTH = 6400

HEAD_PAD = 256
HALO = 16
MIB = 1024 * 1024


def _cparams(sem, vmem_mib):
    return pltpu.CompilerParams(dimension_semantics=sem,
                                vmem_limit_bytes=int(vmem_mib * MIB))


def _rms(xf, g):
    ms = jnp.mean(xf * xf, axis=-1, keepdims=True)
    return xf * lax.rsqrt(ms + EPS) * g


def _norm_rows(dst_ref, dst_off, src_ref, g, rows, chunk):
    def body(c, carry):
        r = pl.multiple_of(c * chunk, chunk)
        y = _rms(src_ref[pl.ds(r, chunk), :].astype(F32), g)
        dst_ref[pl.ds(dst_off + r, chunk), :] = y.astype(dst_ref.dtype)
        return carry
    lax.fori_loop(0, rows // chunk, body, 0)


def _in_proj_kernel(x_ref, g_ref, w_ref, z_ref, h_ref, *, tm):
    @pl.when(pl.program_id(1) == 0)
    def _():
        _norm_rows(h_ref, 0, x_ref, g_ref[...], tm, 128)
    z_ref[...] = jnp.dot(h_ref[...], w_ref[...],
                         preferred_element_type=F32).astype(z_ref.dtype)


def _in_proj(x, g, w, *, tm=1024, tn=1280):
    s, d = x.shape
    n = w.shape[1]
    return pl.pallas_call(
        functools.partial(_in_proj_kernel, tm=tm),
        grid=(s // tm, n // tn),
        in_specs=[pl.BlockSpec((tm, d), lambda i, j: (i, 0)),
                  pl.BlockSpec((1, d), lambda i, j: (0, 0)),
                  pl.BlockSpec((d, tn), lambda i, j: (0, j))],
        out_specs=pl.BlockSpec((tm, tn), lambda i, j: (i, j)),
        out_shape=jax.ShapeDtypeStruct((s, n), BF16),
        scratch_shapes=[pltpu.VMEM((tm, d), BF16)],
        compiler_params=_cparams(("parallel", "arbitrary"), 48),
        name="in_proj",
    )(x, g, w)


def _mla_proj_kernel(ql_ref, ckv_ref, kp_ref, pos_ref, freq_ref, gq_ref, gkv_ref,
                     wq_ref, wkv_ref, q_ref, kn_ref, kpe_ref, v_ref, *, q_scale):
    qn = _rms(ql_ref[...].astype(F32), gq_ref[...]).astype(BF16)
    cn = _rms(ckv_ref[...].astype(F32), gkv_ref[...]).astype(BF16)
    ang = pos_ref[...] * freq_ref[...]
    lane = lax.broadcasted_iota(jnp.int32, (1, 2 * ROPE), 1)
    sign = jnp.where((lane >= ROPE) & (lane < ROPE + ROPE // 2), -1.0, 1.0)
    table = jnp.where(lane < ROPE, jnp.cos(ang), jnp.sin(ang) * sign)
    low = lane < ROPE

    def rot(t):
        t = t * table
        return t + pltpu.roll(t, ROPE, axis=1)

    kpe_ref[...] = jnp.where(low, rot(kp_ref[...].astype(F32)), 0.0).astype(BF16)
    for h in range(N_HEADS):
        qh = jnp.dot(qn, wq_ref[h], preferred_element_type=F32)
        q_ref[h, :, 0:NOPE] = (qh[:, 0:NOPE] * q_scale).astype(BF16)
        q_ref[h, :, NOPE:HEAD_PAD] = (rot(qh[:, NOPE:HEAD_PAD]) * q_scale).astype(BF16)
        kvh = jnp.dot(cn, wkv_ref[h], preferred_element_type=F32)
        kn_ref[h] = kvh[:, 0:NOPE].astype(BF16)
        v_ref[h] = kvh[:, NOPE:HEAD_PAD].astype(BF16)


def _mla_proj(z, pos, freq, gq, gkv, wq, wkv, *, tm=512):
    s = z.shape[0]
    q_scale = (NOPE + ROPE) ** -0.5 * math.log2(math.e)
    def head_out(width):
        return jax.ShapeDtypeStruct((N_HEADS, s, width), BF16)

    def head_spec(width):
        return pl.BlockSpec((N_HEADS, tm, width), lambda i: (0, i, 0))
    const3 = lambda i: (0, 0, 0)
    return pl.pallas_call(
        functools.partial(_mla_proj_kernel, q_scale=q_scale),
        grid=(s // tm,),
        in_specs=[pl.BlockSpec((tm, Q_LORA), lambda i: (i, Z_QLAT // Q_LORA)),
                  pl.BlockSpec((tm, KV_LORA), lambda i: (i, Z_CKV // KV_LORA)),
                  pl.BlockSpec((tm, 2 * ROPE), lambda i: (i, Z_KPE // (2 * ROPE))),
                  pl.BlockSpec((tm, 1), lambda i: (i, 0)),
                  pl.BlockSpec((1, 2 * ROPE), lambda i: (0, 0)),
                  pl.BlockSpec((1, Q_LORA), lambda i: (0, 0)),
                  pl.BlockSpec((1, KV_LORA), lambda i: (0, 0)),
                  pl.BlockSpec((N_HEADS, Q_LORA, HEAD_PAD), const3),
                  pl.BlockSpec((N_HEADS, KV_LORA, HEAD_PAD), const3)],
        out_specs=[head_spec(HEAD_PAD), head_spec(NOPE),
                   pl.BlockSpec((tm, HEAD_PAD - NOPE), lambda i: (i, 0)), head_spec(V_DIM)],
        out_shape=[head_out(HEAD_PAD), head_out(NOPE),
                   jax.ShapeDtypeStruct((s, HEAD_PAD - NOPE), BF16), head_out(V_DIM)],
        compiler_params=_cparams(("parallel",), 48),
        name="mla_proj",
    )(z, z, z, pos, freq, gq, gkv, wq, wkv)


M_INIT = float(np.finfo(np.float32).min)


def _flash_kernel(q_ref, kn_ref, kpe_ref, v_ref, o_ref, s0, s1, p0, p1, a0, a1, m_ref,
                  acc_ref, *, tq, tk, rb, group):
    r = tk // tq
    nb = rb // r
    band = pl.program_id(1)
    row0 = band * rb
    row_end = row0 + rb
    n_plain = r * nb * nb * band + r * (nb * (nb - 1) // 2)
    p0[...] = jnp.zeros(p0.shape, BF16)
    a0[...] = jnp.ones(a0.shape, F32)
    acc_ref[...] = jnp.zeros(acc_ref.shape, F32)
    m_ref[...] = jnp.full(m_ref.shape, M_INIT, F32)
    slots = ((s0, p0, a0), (s1, p1, a1))

    def last_tile(i):
        return lax.div(i, jnp.int32(r))

    def stage_a(row, j, slot, width=tk):
        q = q_ref[0, pl.ds(pl.multiple_of(row * tq, tq), tq), :]
        cols = pl.ds(pl.multiple_of(j * tk, tk), width)
        kt = jnp.concatenate([kn_ref[0, cols, :], kpe_ref[cols, :]], axis=1)
        slots[slot][0][:, 0:width] = lax.dot_general(q, kt, (((1,), (1,)), ((), ())),
                                                     preferred_element_type=F32)

    def stage_b(row, src, dst, width=tk, off=None):
        s = slots[src][0][:, 0:width]
        if off is not None:
            rel = (lax.broadcasted_iota(jnp.int32, (tq, width), 1)
                   - lax.broadcasted_iota(jnp.int32, (tq, width), 0))
            s = jnp.where(rel <= off, s, -jnp.inf)
        m_prev = m_ref[row]
        m_new = jnp.maximum(m_prev, jnp.max(s, axis=-1, keepdims=True))
        slots[dst][2][...] = jnp.exp2(m_prev - m_new)
        slots[dst][1][:, 0:width] = jnp.exp2(s - m_new).astype(BF16)
        m_ref[row] = m_new

    def stage_c(row, j, src, width=tk):
        vt = jnp.concatenate([v_ref[0, pl.ds(pl.multiple_of(j * tk, tk), width), :],
                              jnp.ones((width, HEAD_PAD - V_DIM), BF16)], axis=1)
        acc = slots[src][2][...] * acc_ref[row] + jnp.dot(
            slots[src][1][:, 0:width], vt, preferred_element_type=F32)
        acc_ref[row] = acc
        return acc

    def advance(i, j):
        same_row = j + 1 < last_tile(i)
        more_rows = i + 1 < row_end
        i2 = jnp.where(same_row, i, jnp.where(more_rows, i + 1, row0))
        j2 = jnp.where(same_row, j + 1, jnp.where(more_rows, 0, last_tile(row0)))
        return i2, j2

    first = jnp.maximum(row0, r)
    zero = jnp.int32(0)
    stage_a(first - row0, zero, 0)

    def plain_ticks(state, n):
        ia, ja, ib, jb, ic, jc = state
        for g in range(n):
            cur, prev = (g + 1) % 2, g % 2
            stage_a(ia - row0, ja, cur)
            stage_b(ib - row0, prev, cur)
            stage_c(ic - row0, jc, prev)
            ia, ja, ib, jb, ic, jc = advance(ia, ja) + (ia, ja, ib, jb)
        return ia, ja, ib, jb, ic, jc

    state = advance(first, zero) + (first, zero, first, zero)
    state = plain_ticks(state, (r * (nb * (nb - 1) // 2)) % group)
    state = lax.fori_loop(0, n_plain // group, lambda _, st: plain_ticks(st, group), state)

    _, _, _, _, ic0, jc0 = state
    jb0 = last_tile(row0)

    def edge_tile(k):
        return jb0 + (k // r if isinstance(k, int) else lax.div(k, jnp.int32(r)))

    def edge_tick(k, pos, do_a=True, do_b=True, first_tick=False):
        cur, prev = (pos + 1) % 2, pos % 2
        if do_a:
            stage_a(k + 1, edge_tile(k + 1), cur, ((pos + 1) % r + 1) * tq)
        if do_b:
            stage_b(k, prev, cur, (pos % r + 1) * tq, (pos % r) * tq)
        if first_tick:
            stage_c(ic0 - row0, jc0, prev)
        else:
            acc = stage_c(k - 1, edge_tile(k - 1), prev, ((pos - 1) % r + 1) * tq)
            rows = pl.ds(pl.multiple_of((k - 1) * tq, tq), tq)
            o_ref[rows, :] = (acc[:, 0:V_DIM] / acc[:, V_DIM:HEAD_PAD]).astype(o_ref.dtype)

    period = math.lcm(2, r)
    edge_tick(0, 0, first_tick=True)

    def edge_group(kk, carry):
        for u in range(period):
            edge_tick(1 + kk * period + u, 1 + u)
        return carry
    lax.fori_loop(0, (rb - 2) // period, edge_group, 0)
    for k in range(1 + (rb - 2) // period * period, rb + 1):
        edge_tick(k, k, do_a=k + 1 < rb, do_b=k < rb)


def _flash(q, kn, kpe, v, *, tq=512, tk=1024, rb=8, group=4):
    h, s, _ = q.shape
    band = tq * rb
    r = tk // tq
    nb = rb // r
    assert tk % tq == 0 and rb % r == 0 and s % band == 0
    assert group % 2 == 0 and nb >= 2
    assert (r * nb * nb) % group == 0 and ((r * (nb * (nb - 1) // 2)) % group) % 2 == 0
    return pl.pallas_call(
        functools.partial(_flash_kernel, tq=tq, tk=tk, rb=rb, group=group),
        grid=(h, s // band),
        in_specs=[pl.BlockSpec((1, band, HEAD_PAD), lambda hh, b: (hh, b, 0)),
                  pl.BlockSpec((1, s, NOPE), lambda hh, b: (hh, 0, 0)),
                  pl.BlockSpec((s, HEAD_PAD - NOPE), lambda hh, b: (0, 0),
                               pipeline_mode=pl.Buffered(1)),
                  pl.BlockSpec((1, s, V_DIM), lambda hh, b: (hh, 0, 0))],
        out_specs=pl.BlockSpec((band, V_DIM), lambda hh, b: (b, hh)),
        out_shape=jax.ShapeDtypeStruct((s, h * V_DIM), BF16),
        scratch_shapes=[pltpu.VMEM((tq, tk), F32), pltpu.VMEM((tq, tk), F32),
                        pltpu.VMEM((tq, tk), BF16), pltpu.VMEM((tq, tk), BF16),
                        pltpu.VMEM((tq, 1), F32), pltpu.VMEM((tq, 1), F32),
                        pltpu.VMEM((rb, tq, 1), F32),
                        pltpu.VMEM((rb, tq, HEAD_PAD), F32)],
        compiler_params=_cparams(("parallel", "arbitrary"), 58),
        name="flash",
    )(q, kn, kpe, v)


def _mix_kernel(o_ref, ga_ref, gp_ref, u_ref, uh_ref, x_ref, wab_ref, wpg_ref, ps_ref,
                wpb_ref, wmo_ref, g_ref, out_ref, ext_ref, yp_ref, *, tm):
    i = pl.program_id(0)
    ya = jnp.dot(o_ref[...], wab_ref[...], preferred_element_type=F32)
    ext_ref[0:HALO, :] = jnp.where(i == 0, 0.0, uh_ref[...].astype(F32))
    ext_ref[HALO:HALO + tm, :] = u_ref[...].astype(F32)
    t = i * tm + lax.broadcasted_iota(jnp.int32, (tm, 1), 0)
    for g, w in enumerate(POOL_WINDOWS):
        cols = slice(g * POOL_GROUP_DIM, (g + 1) * POOL_GROUP_DIM)
        cur = ext_ref[HALO:HALO + tm, cols]
        win = cur
        for b in range(1, w):
            win = win + ext_ref[HALO - b:HALO - b + tm, cols]
        cnt = jnp.minimum(t + 1, w).astype(F32)
        d = win / cnt - cur
        yg = jnp.dot(d.astype(BF16), wpg_ref[g], preferred_element_type=F32)
        yp_ref[:, cols] = (yg * ps_ref[:, cols]).astype(BF16)
    yp = jnp.dot(yp_ref[...], wpb_ref[...], preferred_element_type=F32)
    merged = (jax.nn.sigmoid(ga_ref[...].astype(F32)) * ya
              + jax.nn.sigmoid(gp_ref[...].astype(F32)) * yp)
    mo = jnp.dot(merged.astype(BF16), wmo_ref[...], preferred_element_type=F32)
    out_ref[...] = x_ref[...] + _rms(mo, g_ref[...])


def _resident(shape):
    zeros = (0,) * len(shape)
    return pl.BlockSpec(shape, lambda i: zeros, pipeline_mode=pl.Buffered(1))


def _mix(o, z, x, wab, wpg, ps, wpb, wmo, g, *, tm=256):
    s, d = x.shape
    return pl.pallas_call(
        functools.partial(_mix_kernel, tm=tm),
        grid=(s // tm,),
        in_specs=[pl.BlockSpec((tm, ATTN_WIDTH), lambda i: (i, 0)),
                  pl.BlockSpec((tm, d), lambda i: (i, Z_GATE_A // d)),
                  pl.BlockSpec((tm, d), lambda i: (i, Z_GATE_P // d)),
                  pl.BlockSpec((tm, POOL_WIDTH), lambda i: (i, Z_POOL // POOL_WIDTH)),
                  pl.BlockSpec((HALO, POOL_WIDTH),
                               lambda i: (jnp.maximum(i * (tm // HALO) - 1, 0),
                                          Z_POOL // POOL_WIDTH)),
                  pl.BlockSpec((tm, d), lambda i: (i, 0)),
                  _resident(wab.shape), _resident(wpg.shape), _resident(ps.shape),
                  _resident(wpb.shape), _resident(wmo.shape), _resident(g.shape)],
        out_specs=pl.BlockSpec((tm, d), lambda i: (i, 0)),
        out_shape=jax.ShapeDtypeStruct((s, d), F32),
        scratch_shapes=[pltpu.VMEM((HALO + tm, POOL_WIDTH), F32),
                        pltpu.VMEM((tm, POOL_WIDTH), BF16)],
        compiler_params=_cparams(("parallel",), 56),
        name="mix",
    )(o, z, z, z, z, x, wab, wpg, ps, wpb, wmo, g)


def _mem_kv_kernel(mem_ref, g_ref, w_ref, kv_ref):
    mn = _rms(mem_ref[...], g_ref[...]).astype(BF16)
    kv_ref[...] = jnp.dot(mn, w_ref[...], preferred_element_type=F32).astype(kv_ref.dtype)


def _mem_kv(mem, g, w, *, tn=1024):
    m, d = mem.shape
    n = w.shape[1]
    return pl.pallas_call(
        _mem_kv_kernel,
        grid=(n // tn,),
        in_specs=[pl.BlockSpec((m, d), lambda j: (0, 0)),
                  pl.BlockSpec((1, d), lambda j: (0, 0)),
                  pl.BlockSpec((d, tn), lambda j: (0, j))],
        out_specs=pl.BlockSpec((m, tn), lambda j: (0, j)),
        out_shape=jax.ShapeDtypeStruct((m, n), BF16),
        compiler_params=_cparams(("parallel",), 32),
        name="mem_kv",
    )(mem, g, w)


def _mem_attn_kernel(x_ref, gpre_ref, wq_ref, kv_ref, wo_ref, gpost_ref, out_ref,
                     o_ref):
    x = x_ref[...]
    h = _rms(x, gpre_ref[...]).astype(BF16)
    scale = MEM_HEAD_DIM ** -0.5
    q = (jnp.dot(h, wq_ref[...], preferred_element_type=F32) * scale).astype(BF16)
    for hd in range(MEM_HEADS):
        cols = slice(hd * MEM_HEAD_DIM, (hd + 1) * MEM_HEAD_DIM)
        vcols = slice(D_MODEL + hd * MEM_HEAD_DIM, D_MODEL + (hd + 1) * MEM_HEAD_DIM)
        s = lax.dot_general(q[:, cols], kv_ref[:, cols], (((1,), (1,)), ((), ())),
                            preferred_element_type=F32)
        p = jnp.exp(s - jnp.max(s, axis=-1, keepdims=True))
        l = jnp.sum(p, axis=-1, keepdims=True)
        oh = jnp.dot(p.astype(BF16), kv_ref[:, vcols], preferred_element_type=F32)
        o_ref[:, cols] = (oh / l).astype(BF16)
    y = jnp.dot(o_ref[...], wo_ref[...], preferred_element_type=F32)
    out_ref[...] = x + _rms(y, gpost_ref[...])


def _mem_attn(x, gpre, wq, kv, wo, gpost, *, tm=256):
    s, d = x.shape
    return pl.pallas_call(
        _mem_attn_kernel,
        grid=(s // tm,),
        in_specs=[pl.BlockSpec((tm, d), lambda i: (i, 0)),
                  _resident(gpre.shape), _resident(wq.shape), _resident(kv.shape),
                  _resident(wo.shape), _resident(gpost.shape)],
        out_specs=pl.BlockSpec((tm, d), lambda i: (i, 0)),
        out_shape=jax.ShapeDtypeStruct((s, d), F32),
        scratch_shapes=[pltpu.VMEM((tm, d), BF16)],
        compiler_params=_cparams(("parallel",), 48),
        name="mem_attn",
    )(x, gpre, wq, kv, wo, gpost)


def _gelu_tanh(x):
    c = math.sqrt(2.0 / math.pi)
    return 0.5 * x * (1.0 + jnp.tanh(c * (x + 0.044715 * (x * x * x))))


def _ffn_kernel(x_ref, xh_ref, gpre_ref, wg_ref, wu_ref, cwg_ref, cwu_ref, cbg_ref,
                cbu_ref, wo_ref, gpost_ref, out_ref, h_ref, a_ref, acc_ref, *, tm, tn):
    i = pl.program_id(0)
    j = pl.program_id(1)

    @pl.when(j == 0)
    def _():
        hh = _rms(xh_ref[...], gpre_ref[...])
        h_ref[0:HALO, :] = jnp.where(i == 0, 0.0, hh).astype(BF16)
        _norm_rows(h_ref, HALO, x_ref, gpre_ref[...], tm, 128)
        acc_ref[...] = jnp.zeros(acc_ref.shape, F32)

    def conv(w_ref, cw_ref, cb_ref):
        a_ref[...] = jnp.dot(h_ref[...], w_ref[...], preferred_element_type=F32)
        return (cb_ref[...]
                + a_ref[HALO - 2:HALO - 2 + tm, :] * cw_ref[0:1, :]
                + a_ref[HALO - 1:HALO - 1 + tm, :] * cw_ref[1:2, :]
                + a_ref[HALO:HALO + tm, :] * cw_ref[2:3, :])

    gate = conv(wg_ref, cwg_ref, cbg_ref)
    act = _gelu_tanh(gate)
    up = conv(wu_ref, cwu_ref, cbu_ref)
    acc_ref[...] += jnp.dot((act * up).astype(BF16), wo_ref[...],
                            preferred_element_type=F32)

    @pl.when(j == pl.num_programs(1) - 1)
    def _():
        out_ref[...] = x_ref[...] + _rms(acc_ref[...], gpost_ref[...])


def _ffn(x, gpre, w_in, conv_w, conv_b, w_out, gpost, *, tm=512, tn=512):
    s, d = x.shape
    nj = D_FF // tn
    return pl.pallas_call(
        functools.partial(_ffn_kernel, tm=tm, tn=tn),
        grid=(s // tm, nj),
        in_specs=[pl.BlockSpec((tm, d), lambda i, j: (i, 0)),
                  pl.BlockSpec((HALO, d),
                               lambda i, j: (jnp.maximum(i * (tm // HALO) - 1, 0), 0)),
                  pl.BlockSpec((1, d), lambda i, j: (0, 0)),
                  pl.BlockSpec((d, tn), lambda i, j: (0, j)),
                  pl.BlockSpec((d, tn), lambda i, j: (0, nj + j)),
                  pl.BlockSpec((3, tn), lambda i, j: (0, j)),
                  pl.BlockSpec((3, tn), lambda i, j: (0, nj + j)),
                  pl.BlockSpec((1, tn), lambda i, j: (0, j)),
                  pl.BlockSpec((1, tn), lambda i, j: (0, nj + j)),
                  pl.BlockSpec((tn, d), lambda i, j: (j, 0)),
                  pl.BlockSpec((1, d), lambda i, j: (0, 0))],
        out_specs=pl.BlockSpec((tm, d), lambda i, j: (i, 0)),
        out_shape=jax.ShapeDtypeStruct((s, d), F32),
        scratch_shapes=[pltpu.VMEM((HALO + tm, d), BF16),
                        pltpu.VMEM((HALO + tm, tn), F32),
                        pltpu.VMEM((tm, d), F32)],
        compiler_params=_cparams(("parallel", "arbitrary"), 48),
        name="ffn",
    )(x, x, gpre, w_in, w_in, conv_w, conv_w, conv_b, conv_b, w_out, gpost)


def _swap_halves(w):
    half = ROPE // 2
    return jnp.concatenate([w[..., half:], w[..., :half]], axis=-1)


def _prep_w_in(w_in):
    q_lat, c_kv, k_pe, u_pool, gate_a, gate_p = jnp.split(
        w_in, np.cumsum([Q_LORA, KV_LORA, ROPE, POOL_WIDTH, D_MODEL])[:].tolist(), axis=1)
    pad = jnp.zeros((D_MODEL, Z_WIDTH - Z_KPE - 2 * ROPE), w_in.dtype)
    return jnp.concatenate([gate_a, gate_p, u_pool, q_lat, c_kv, k_pe,
                            _swap_halves(k_pe), pad], axis=1).astype(BF16)


def _prep_w_q(w_q_b):
    w = w_q_b.reshape(Q_LORA, N_HEADS, NOPE + ROPE).transpose(1, 0, 2)
    pe = w[..., NOPE:]
    return jnp.concatenate([w, _swap_halves(pe)], axis=-1).astype(BF16)


def _prep_w_kv(w_kv_b):
    return w_kv_b.reshape(KV_LORA, N_HEADS, NOPE + V_DIM).transpose(1, 0, 2).astype(BF16)


def kernel(x, mem, positions, g_mix_pre, w_in, g_q_a, w_q_b, g_kv_a, w_kv_b, w_attn_br,
           w_pool_grp, pool_scale, w_pool_br, w_mix_out, g_mix_post, g_mem_pre, g_mem_kv,
           w_mem_q, w_mem_kv, w_mem_o, g_mem_post, g_ffn_pre, w_ffn_in, conv_w, conv_b,
           w_ffn_out, g_ffn_post):
    batch, seq, d = x.shape
    depth = w_in.shape[0]
    assert (batch, seq, d) == (1, SEQ, D_MODEL)
    inv_freq = ROPE_THETA ** (-np.arange(0, ROPE, 2, dtype=np.float32) / ROPE)
    freq = jnp.asarray(np.tile(inv_freq.astype(np.float32), 4)[None, :])
    pos = positions.reshape(seq, 1).astype(F32)
    xs = x.reshape(seq, d)
    mems = mem.reshape(N_MEM, d)
    for l in range(depth):
        z = _in_proj(xs, g_mix_pre[l][None], _prep_w_in(w_in[l]))
        q, kn, kpe, v = _mla_proj(z, pos, freq, g_q_a[l][None], g_kv_a[l][None],
                                  _prep_w_q(w_q_b[l]), _prep_w_kv(w_kv_b[l]))
        o = _flash(q, kn, kpe, v)
        x1 = _mix(o, z, xs, w_attn_br[l].astype(BF16), w_pool_grp[l].astype(BF16),
                  pool_scale[l][None], w_pool_br[l].astype(BF16),
                  w_mix_out[l].astype(BF16), g_mix_post[l][None])
        kv = _mem_kv(mems, g_mem_kv[l][None], w_mem_kv[l].astype(BF16))
        x2 = _mem_attn(x1, g_mem_pre[l][None], w_mem_q[l].astype(BF16), kv,
                       w_mem_o[l].astype(BF16), g_mem_post[l][None])
        xs = _ffn(x2, g_ffn_pre[l][None], w_ffn_in[l].astype(BF16), conv_w[l],
                  conv_b[l][None], w_ffn_out[l].astype(BF16), g_ffn_post[l][None])
    return xs.reshape(batch, seq, d)
```

```python
import functools
import math

import numpy as np
import jax
import jax.numpy as jnp
from jax import lax
from jax.experimental import pallas as pl
from jax.experimental.pallas import tpu as pltpu

F32 = jnp.float32
BF16 = jnp.bfloat16

D_MODEL = 2048
SEQ = 16384
N_HEADS = 16
Q_LORA = 512
KV_LORA = 512
NOPE = 128
ROPE = 64
V_DIM = 128
ROPE_THETA = 10000.0
ATTN_WIDTH = N_HEADS * V_DIM
POOL_WIDTH = 1024
POOL_WINDOWS = (2, 4, 8, 16)
POOL_GROUP_DIM = 256
N_MEM = 256
MEM_HEADS = 4
MEM_HEAD_DIM = 512
D_FF = 5632
EPS = 1e-6

LAT_Q = 0
LAT_KV = 512
LAT_KPE = 1024
LAT_WIDTH = 1280

HEAD_PAD = 256
HALO = 16
MIB = 1024 * 1024


def _cparams(sem, vmem_mib):
    return pltpu.CompilerParams(dimension_semantics=sem,
                                vmem_limit_bytes=int(vmem_mib * MIB))


def _rms(xf, g):
    ms = jnp.mean(xf * xf, axis=-1, keepdims=True)
    return xf * lax.rsqrt(ms + EPS) * g


def _norm_rows(dst_ref, dst_off, src_ref, g, rows, chunk):
    def body(c, carry):
        r = pl.multiple_of(c * chunk, chunk)
        y = _rms(src_ref[pl.ds(r, chunk), :].astype(F32), g)
        dst_ref[pl.ds(dst_off + r, chunk), :] = y.astype(dst_ref.dtype)
        return carry
    lax.fori_loop(0, rows // chunk, body, 0)


def _in_proj_kernel(x_ref, g_ref, wg_ref, wp_ref, wl_ref, zg_ref, zp_ref, zl_ref, h_ref,
                    *, tm, gate_tiles):
    j = pl.program_id(1)

    @pl.when(j == 0)
    def _():
        _norm_rows(h_ref, 0, x_ref, g_ref[...], tm, 128)

    def project(w_ref, z_ref):
        z_ref[...] = jnp.dot(h_ref[...], w_ref[...],
                             preferred_element_type=F32).astype(z_ref.dtype)

    pl.when(j < gate_tiles)(lambda: project(wg_ref, zg_ref))
    pl.when(j == gate_tiles)(lambda: project(wp_ref, zp_ref))
    pl.when(j == gate_tiles + 1)(lambda: project(wl_ref, zl_ref))


def _in_proj(x, g, w_gates, w_pool, w_lat, *, tm=1024, tn=1024):
    s, d = x.shape
    gate_tiles = w_gates.shape[1] // tn
    n_pool, n_lat = w_pool.shape[1], w_lat.shape[1]

    def gate_tile(j):
        return jnp.minimum(j, gate_tiles - 1)
    return pl.pallas_call(
        functools.partial(_in_proj_kernel, tm=tm, gate_tiles=gate_tiles),
        grid=(s // tm, gate_tiles + 2),
        in_specs=[pl.BlockSpec((tm, d), lambda i, j: (i, 0)),
                  pl.BlockSpec((1, d), lambda i, j: (0, 0)),
                  pl.BlockSpec((d, tn), lambda i, j: (0, gate_tile(j))),
                  pl.BlockSpec((d, n_pool), lambda i, j: (0, 0), pipeline_mode=pl.Buffered(1)),
                  pl.BlockSpec((d, n_lat), lambda i, j: (0, 0), pipeline_mode=pl.Buffered(1))],
        out_specs=[pl.BlockSpec((tm, tn), lambda i, j: (i, gate_tile(j))),
                   pl.BlockSpec((tm, n_pool), lambda i, j: (i, 0)),
                   pl.BlockSpec((tm, n_lat), lambda i, j: (i, 0))],
        out_shape=[jax.ShapeDtypeStruct((s, w_gates.shape[1]), BF16),
                   jax.ShapeDtypeStruct((s, n_pool), BF16),
                   jax.ShapeDtypeStruct((s, n_lat), BF16)],
        scratch_shapes=[pltpu.VMEM((tm, d), BF16)],
        compiler_params=_cparams(("parallel", "arbitrary"), 58),
        name="in_proj",
    )(x, g, w_gates, w_pool, w_lat)


def _mla_proj_kernel(ql_ref, ckv_ref, kp_ref, pos_ref, freq_ref, gq_ref, gkv_ref,
                     wq_ref, wkv_ref, q_ref, kn_ref, kpe_ref, v_ref, *, q_scale):
    qn = _rms(ql_ref[...].astype(F32), gq_ref[...]).astype(BF16)
    cn = _rms(ckv_ref[...].astype(F32), gkv_ref[...]).astype(BF16)
    ang = pos_ref[...] * freq_ref[...]
    lane = lax.broadcasted_iota(jnp.int32, (1, 2 * ROPE), 1)
    sign = jnp.where((lane >= ROPE) & (lane < ROPE + ROPE // 2), -1.0, 1.0)
    table = jnp.where(lane < ROPE, jnp.cos(ang), jnp.sin(ang) * sign)
    low = lane < ROPE

    def rot(t):
        t = t * table
        return t + pltpu.roll(t, ROPE, axis=1)

    kpe_ref[...] = jnp.where(low, rot(kp_ref[...].astype(F32)), 0.0).astype(BF16)
    for h in range(N_HEADS):
        qh = jnp.dot(qn, wq_ref[h], preferred_element_type=F32)
        q_ref[h, :, 0:NOPE] = (qh[:, 0:NOPE] * q_scale).astype(BF16)
        q_ref[h, :, NOPE:HEAD_PAD] = (rot(qh[:, NOPE:HEAD_PAD]) * q_scale).astype(BF16)
        kvh = jnp.dot(cn, wkv_ref[h], preferred_element_type=F32)
        kn_ref[h] = kvh[:, 0:NOPE].astype(BF16)
        v_ref[h] = kvh[:, NOPE:HEAD_PAD].astype(BF16)


def _mla_proj(z_lat, pos, freq, gq, gkv, wq, wkv, *, tm=512):
    s = z_lat.shape[0]
    q_scale = (NOPE + ROPE) ** -0.5 * math.log2(math.e)

    def head_out(width):
        return jax.ShapeDtypeStruct((N_HEADS, s, width), BF16)

    def head_spec(width):
        return pl.BlockSpec((N_HEADS, tm, width), lambda i: (0, i, 0))
    const3 = lambda i: (0, 0, 0)
    return pl.pallas_call(
        functools.partial(_mla_proj_kernel, q_scale=q_scale),
        grid=(s // tm,),
        in_specs=[pl.BlockSpec((tm, Q_LORA), lambda i: (i, LAT_Q // Q_LORA)),
                  pl.BlockSpec((tm, KV_LORA), lambda i: (i, LAT_KV // KV_LORA)),
                  pl.BlockSpec((tm, 2 * ROPE), lambda i: (i, LAT_KPE // (2 * ROPE))),
                  pl.BlockSpec((tm, 1), lambda i: (i, 0)),
                  pl.BlockSpec((1, 2 * ROPE), lambda i: (0, 0)),
                  pl.BlockSpec((1, Q_LORA), lambda i: (0, 0)),
                  pl.BlockSpec((1, KV_LORA), lambda i: (0, 0)),
                  pl.BlockSpec((N_HEADS, Q_LORA, HEAD_PAD), const3),
                  pl.BlockSpec((N_HEADS, KV_LORA, HEAD_PAD), const3)],
        out_specs=[head_spec(HEAD_PAD), head_spec(NOPE),
                   pl.BlockSpec((tm, HEAD_PAD - NOPE), lambda i: (i, 0)), head_spec(V_DIM)],
        out_shape=[head_out(HEAD_PAD), head_out(NOPE),
                   jax.ShapeDtypeStruct((s, HEAD_PAD - NOPE), BF16), head_out(V_DIM)],
        compiler_params=_cparams(("parallel",), 48),
        name="mla_proj",
    )(z_lat, z_lat, z_lat, pos, freq, gq, gkv, wq, wkv)


M_INIT = float(np.finfo(np.float32).min)
EDGE_BLOCK = 3


def _flash_kernel(q_ref, kn_ref, kpe_ref, v_ref, o_ref, *scratch, tq, tk, rb, group, lag):
    n_s = 2 * lag
    s_bufs, (p0, p1, a0, a1, m_ref, acc_ref) = scratch[:n_s], scratch[n_s:]
    p_bufs, a_bufs = (p0, p1), (a0, a1)
    r = tk // tq
    nb = rb // r
    band = pl.program_id(1)
    row0 = band * rb
    row_end = row0 + rb
    n_plain = r * nb * nb * band + r * (nb * (nb - 1) // 2)
    p1[...] = jnp.zeros(p1.shape, BF16)
    a1[...] = jnp.ones(a1.shape, F32)
    acc_ref[...] = jnp.zeros(acc_ref.shape, F32)
    m_ref[...] = jnp.full(m_ref.shape, M_INIT, F32)

    def last_tile(i):
        return lax.div(i, jnp.int32(r))

    def stage_a(e, row, j, width=tk):
        q = q_ref[0, pl.ds(pl.multiple_of(row * tq, tq), tq), :]
        cols = pl.ds(pl.multiple_of(j * tk, tk), width)
        kt = jnp.concatenate([kn_ref[0, cols, :], kpe_ref[cols, :]], axis=1)
        s_bufs[e % n_s][:, 0:width] = lax.dot_general(q, kt, (((1,), (1,)), ((), ())),
                                                      preferred_element_type=F32)

    def stage_b(e, row, width=tk, off=None):
        s = s_bufs[e % n_s][:, 0:width]
        if off is not None:
            rel = (lax.broadcasted_iota(jnp.int32, (tq, width), 1)
                   - lax.broadcasted_iota(jnp.int32, (tq, width), 0))
            s = jnp.where(rel <= off, s, -jnp.inf)
        m_prev = m_ref[row]
        m_new = jnp.maximum(m_prev, jnp.max(s, axis=-1, keepdims=True))
        a_bufs[e % 2][...] = jnp.exp2(m_prev - m_new)
        p_bufs[e % 2][:, 0:width] = jnp.exp2(s - m_new).astype(BF16)
        m_ref[row] = m_new

    def stage_c(e, row, j, width=tk):
        vt = jnp.concatenate([v_ref[0, pl.ds(pl.multiple_of(j * tk, tk), width), :],
                              jnp.ones((width, HEAD_PAD - V_DIM), BF16)], axis=1)
        acc = a_bufs[e % 2][...] * acc_ref[row] + jnp.dot(
            p_bufs[e % 2][:, 0:width], vt, preferred_element_type=F32)
        acc_ref[row] = acc
        return acc

    def advance(i, j):
        lt = last_tile(i)
        same_row = j + 1 < lt
        more_rows = i + 1 < row_end
        i_plain = jnp.where(same_row, i, jnp.where(more_rows, i + 1, row0))
        j_plain = jnp.where(same_row, j + 1, jnp.where(more_rows, 0, last_tile(row0)))
        on_edge = j == lt
        return (jnp.where(on_edge, i + 1, i_plain),
                jnp.where(on_edge, last_tile(i + 1), j_plain))

    first = jnp.maximum(row0, r)
    pairs = [(first, jnp.int32(0))]
    for _ in range(lag + 1):
        pairs.append(advance(*pairs[-1]))
    for e in range(lag):
        stage_a(e, pairs[e][0] - row0, pairs[e][1])

    def plain_group(_, state):
        state = list(state)
        for g in range(group):
            (ia, ja), (ib, jb), (ic, jc) = state[0], state[lag], state[lag + 1]
            stage_a(lag + g, ia - row0, ja)
            stage_b(g, ib - row0)
            stage_c(g - 1, ic - row0, jc)
            state = [advance(ia, ja)] + state[:-1]
        return tuple(state)

    state = tuple(pairs[lag - d] for d in range(lag + 1)) + (pairs[0],)
    state = lax.fori_loop(0, n_plain // group, plain_group, state)

    ic0, jc0 = state[lag + 1]
    jb0 = last_tile(row0)

    def edge_tick(k):
        if lag <= k + lag < rb:
            ka = k + lag
            stage_a(ka, ka, jb0 + ka // r, (ka % r + 1) * tq)
        if k < rb:
            stage_b(k, k, (k % r + 1) * tq, (k % r) * tq)
        if k == 0:
            stage_c(-1, ic0 - row0, jc0)
        else:
            kc = k - 1
            acc = stage_c(kc, kc, jb0 + kc // r, (kc % r + 1) * tq)
            o_ref[kc * tq:k * tq, :] = (
                acc[:, 0:V_DIM] / acc[:, V_DIM:HEAD_PAD]).astype(o_ref.dtype)

    for k0 in range(0, rb + 1, EDGE_BLOCK):
        @pl.when(band >= 0)
        def _():
            for k in range(k0, min(k0 + EDGE_BLOCK, rb + 1)):
                edge_tick(k)


def _flash(q, kn, kpe, v, *, tq=512, tk=1024, rb=8, group=4, lag=1):
    h, s, _ = q.shape
    band = tq * rb
    r = tk // tq
    nb = rb // r
    assert tk % tq == 0 and rb % r == 0 and s % band == 0
    assert group % (2 * lag) == 0 and nb >= 2 and lag <= rb
    assert (r * nb * nb) % group == 0 and (r * (nb * (nb - 1) // 2)) % group == 0
    return pl.pallas_call(
        functools.partial(_flash_kernel, tq=tq, tk=tk, rb=rb, group=group, lag=lag),
        grid=(h, s // band),
        in_specs=[pl.BlockSpec((1, band, HEAD_PAD), lambda hh, b: (hh, b, 0)),
                  pl.BlockSpec((1, s, NOPE), lambda hh, b: (hh, 0, 0)),
                  pl.BlockSpec((s, HEAD_PAD - NOPE), lambda hh, b: (0, 0),
                               pipeline_mode=pl.Buffered(1)),
                  pl.BlockSpec((1, s, V_DIM), lambda hh, b: (hh, 0, 0))],
        out_specs=pl.BlockSpec((band, V_DIM), lambda hh, b: (b, hh)),
        out_shape=jax.ShapeDtypeStruct((s, h * V_DIM), BF16),
        scratch_shapes=[pltpu.VMEM((tq, tk), F32)] * (2 * lag) + [
                        pltpu.VMEM((tq, tk), BF16), pltpu.VMEM((tq, tk), BF16),
                        pltpu.VMEM((tq, 1), F32), pltpu.VMEM((tq, 1), F32),
                        pltpu.VMEM((rb, tq, 1), F32),
                        pltpu.VMEM((rb, tq, HEAD_PAD), F32)],
        compiler_params=_cparams(("parallel", "arbitrary"), 58),
        name="flash",
    )(q, kn, kpe, v)


def _mix_kernel(o_ref, ga_ref, gp_ref, u_ref, uh_ref, x_ref, wab_ref, wpg_ref, ps_ref,
                wpb_ref, wmo_ref, g_ref, out_ref, ext_ref, yp_ref, *, tm):
    i = pl.program_id(0)
    ya = jnp.dot(o_ref[...], wab_ref[...], preferred_element_type=F32)
    ext_ref[0:HALO, :] = jnp.where(i == 0, 0.0, uh_ref[...].astype(F32))
    ext_ref[HALO:HALO + tm, :] = u_ref[...].astype(F32)
    t = i * tm + lax.broadcasted_iota(jnp.int32, (tm, 1), 0)
    for g, w in enumerate(POOL_WINDOWS):
        cols = slice(g * POOL_GROUP_DIM, (g + 1) * POOL_GROUP_DIM)
        cur = ext_ref[HALO:HALO + tm, cols]
        win = cur
        for b in range(1, w):
            win = win + ext_ref[HALO - b:HALO - b + tm, cols]
        cnt = jnp.minimum(t + 1, w).astype(F32)
        d = win / cnt - cur
        yg = jnp.dot(d.astype(BF16), wpg_ref[g], preferred_element_type=F32)
        yp_ref[:, cols] = (yg * ps_ref[:, cols]).astype(BF16)
    yp = jnp.dot(yp_ref[...], wpb_ref[...], preferred_element_type=F32)
    merged = (jax.nn.sigmoid(ga_ref[...].astype(F32)) * ya
              + jax.nn.sigmoid(gp_ref[...].astype(F32)) * yp)
    mo = jnp.dot(merged.astype(BF16), wmo_ref[...], preferred_element_type=F32)
    out_ref[...] = x_ref[...] + _rms(mo, g_ref[...])


def _resident(shape):
    zeros = (0,) * len(shape)
    return pl.BlockSpec(shape, lambda i: zeros, pipeline_mode=pl.Buffered(1))


def _mix(o, z_gates, z_pool, x, wab, wpg, ps, wpb, wmo, g, *, tm=256):
    s, d = x.shape
    return pl.pallas_call(
        functools.partial(_mix_kernel, tm=tm),
        grid=(s // tm,),
        in_specs=[pl.BlockSpec((tm, ATTN_WIDTH), lambda i: (i, 0)),
                  pl.BlockSpec((tm, d), lambda i: (i, 0)),
                  pl.BlockSpec((tm, d), lambda i: (i, 1)),
                  pl.BlockSpec((tm, POOL_WIDTH), lambda i: (i, 0)),
                  pl.BlockSpec((HALO, POOL_WIDTH),
                               lambda i: (jnp.maximum(i * (tm // HALO) - 1, 0), 0)),
                  pl.BlockSpec((tm, d), lambda i: (i, 0)),
                  _resident(wab.shape), _resident(wpg.shape), _resident(ps.shape),
                  _resident(wpb.shape), _resident(wmo.shape), _resident(g.shape)],
        out_specs=pl.BlockSpec((tm, d), lambda i: (i, 0)),
        out_shape=jax.ShapeDtypeStruct((s, d), F32),
        scratch_shapes=[pltpu.VMEM((HALO + tm, POOL_WIDTH), F32),
                        pltpu.VMEM((tm, POOL_WIDTH), BF16)],
        compiler_params=_cparams(("parallel",), 56),
        name="mix",
    )(o, z_gates, z_gates, z_pool, z_pool, x, wab, wpg, ps, wpb, wmo, g)


def _mem_kv_kernel(mem_ref, g_ref, w_ref, kv_ref):
    mn = _rms(mem_ref[...], g_ref[...]).astype(BF16)
    kv_ref[...] = jnp.dot(mn, w_ref[...], preferred_element_type=F32).astype(kv_ref.dtype)


def _mem_kv(mem, g, w, *, tn=1024):
    m, d = mem.shape
    n = w.shape[1]
    return pl.pallas_call(
        _mem_kv_kernel,
        grid=(n // tn,),
        in_specs=[pl.BlockSpec((m, d), lambda j: (0, 0)),
                  pl.BlockSpec((1, d), lambda j: (0, 0)),
                  pl.BlockSpec((d, tn), lambda j: (0, j))],
        out_specs=pl.BlockSpec((m, tn), lambda j: (0, j)),
        out_shape=jax.ShapeDtypeStruct((m, n), BF16),
        compiler_params=_cparams(("parallel",), 32),
        name="mem_kv",
    )(mem, g, w)


def _mem_attn_kernel(x_ref, gpre_ref, wq_ref, kv_ref, wo_ref, gpost_ref, out_ref,
                     o_ref):
    x = x_ref[...]
    h = _rms(x, gpre_ref[...]).astype(BF16)
    scale = MEM_HEAD_DIM ** -0.5
    q = (jnp.dot(h, wq_ref[...], preferred_element_type=F32) * scale).astype(BF16)
    for hd in range(MEM_HEADS):
        cols = slice(hd * MEM_HEAD_DIM, (hd + 1) * MEM_HEAD_DIM)
        vcols = slice(D_MODEL + hd * MEM_HEAD_DIM, D_MODEL + (hd + 1) * MEM_HEAD_DIM)
        s = lax.dot_general(q[:, cols], kv_ref[:, cols], (((1,), (1,)), ((), ())),
                            preferred_element_type=F32)
        p = jnp.exp(s - jnp.max(s, axis=-1, keepdims=True))
        l = jnp.sum(p, axis=-1, keepdims=True)
        oh = jnp.dot(p.astype(BF16), kv_ref[:, vcols], preferred_element_type=F32)
        o_ref[:, cols] = (oh / l).astype(BF16)
    y = jnp.dot(o_ref[...], wo_ref[...], preferred_element_type=F32)
    out_ref[...] = x + _rms(y, gpost_ref[...])


def _mem_attn(x, gpre, wq, kv, wo, gpost, *, tm=256):
    s, d = x.shape
    return pl.pallas_call(
        _mem_attn_kernel,
        grid=(s // tm,),
        in_specs=[pl.BlockSpec((tm, d), lambda i: (i, 0)),
                  _resident(gpre.shape), _resident(wq.shape), _resident(kv.shape),
                  _resident(wo.shape), _resident(gpost.shape)],
        out_specs=pl.BlockSpec((tm, d), lambda i: (i, 0)),
        out_shape=jax.ShapeDtypeStruct((s, d), F32),
        scratch_shapes=[pltpu.VMEM((tm, d), BF16)],
        compiler_params=_cparams(("parallel",), 48),
        name="mem_attn",
    )(x, gpre, wq, kv, wo, gpost)


def _gelu_tanh(x):
    c = math.sqrt(2.0 / math.pi)
    return 0.5 * x * (1.0 + jnp.tanh(c * (x + 0.044715 * (x * x * x))))


def _ffn_kernel(x_ref, xh_ref, gpre_ref, wg_ref, wu_ref, cwg_ref, cwu_ref, cbg_ref,
                cbu_ref, wo_ref, gpost_ref, out_ref, h_ref, a_ref, *, tm, tn):
    i = pl.program_id(0)
    j = pl.program_id(1)

    @pl.when(j == 0)
    def _():
        hh = _rms(xh_ref[...], gpre_ref[...])
        h_ref[0:HALO, :] = jnp.where(i == 0, 0.0, hh).astype(BF16)
        _norm_rows(h_ref, HALO, x_ref, gpre_ref[...], tm, 128)
        out_ref[...] = jnp.zeros(out_ref.shape, F32)

    def conv(w_ref, cw_ref, cb_ref):
        a_ref[...] = jnp.dot(h_ref[...], w_ref[...], preferred_element_type=F32)
        return (cb_ref[...]
                + a_ref[HALO - 2:HALO - 2 + tm, :] * cw_ref[0:1, :]
                + a_ref[HALO - 1:HALO - 1 + tm, :] * cw_ref[1:2, :]
                + a_ref[HALO:HALO + tm, :] * cw_ref[2:3, :])

    gate = conv(wg_ref, cwg_ref, cbg_ref)
    act = _gelu_tanh(gate)
    up = conv(wu_ref, cwu_ref, cbu_ref)
    out_ref[...] += jnp.dot((act * up).astype(BF16), wo_ref[...],
                            preferred_element_type=F32)

    @pl.when(j == pl.num_programs(1) - 1)
    def _():
        out_ref[...] = x_ref[...] + _rms(out_ref[...], gpost_ref[...])


def _ffn(x, gpre, w_in, conv_w, conv_b, w_out, gpost, *, tm=512, tn=512):
    s, d = x.shape
    nj = D_FF // tn
    return pl.pallas_call(
        functools.partial(_ffn_kernel, tm=tm, tn=tn),
        grid=(s // tm, nj),
        in_specs=[pl.BlockSpec((tm, d), lambda i, j: (i, 0)),
                  pl.BlockSpec((HALO, d),
                               lambda i, j: (jnp.maximum(i * (tm // HALO) - 1, 0), 0)),
                  pl.BlockSpec((1, d), lambda i, j: (0, 0)),
                  pl.BlockSpec((d, tn), lambda i, j: (0, j)),
                  pl.BlockSpec((d, tn), lambda i, j: (0, nj + j)),
                  pl.BlockSpec((3, tn), lambda i, j: (0, j)),
                  pl.BlockSpec((3, tn), lambda i, j: (0, nj + j)),
                  pl.BlockSpec((1, tn), lambda i, j: (0, j)),
                  pl.BlockSpec((1, tn), lambda i, j: (0, nj + j)),
                  pl.BlockSpec((tn, d), lambda i, j: (j, 0)),
                  pl.BlockSpec((1, d), lambda i, j: (0, 0))],
        out_specs=pl.BlockSpec((tm, d), lambda i, j: (i, 0)),
        out_shape=jax.ShapeDtypeStruct((s, d), F32),
        scratch_shapes=[pltpu.VMEM((HALO + tm, d), BF16),
                        pltpu.VMEM((HALO + tm, tn), F32)],
        compiler_params=_cparams(("parallel", "arbitrary"), 48),
        name="ffn",
    )(x, x, gpre, w_in, w_in, conv_w, conv_w, conv_b, conv_b, w_out, gpost)


def _swap_halves(w):
    half = ROPE // 2
    return jnp.concatenate([w[..., half:], w[..., :half]], axis=-1)


def _prep_w_in(w_in):
    lat_end = Q_LORA + KV_LORA + ROPE
    w = w_in.astype(BF16)
    lat, k_pe = w[:, :lat_end], w[:, Q_LORA + KV_LORA:lat_end]
    pad = jnp.zeros((D_MODEL, LAT_WIDTH - LAT_KPE - 2 * ROPE), BF16)
    w_lat = jnp.concatenate([lat, _swap_halves(k_pe), pad], axis=1)
    return w[:, lat_end + POOL_WIDTH:], w[:, lat_end:lat_end + POOL_WIDTH], w_lat


def _prep_w_q(w_q_b):
    w = w_q_b.reshape(Q_LORA, N_HEADS, NOPE + ROPE).transpose(1, 0, 2)
    pe = w[..., NOPE:]
    return jnp.concatenate([w, _swap_halves(pe)], axis=-1).astype(BF16)


def _prep_w_kv(w_kv_b):
    return w_kv_b.reshape(KV_LORA, N_HEADS, NOPE + V_DIM).transpose(1, 0, 2).astype(BF16)


def kernel(x, mem, positions, g_mix_pre, w_in, g_q_a, w_q_b, g_kv_a, w_kv_b, w_attn_br,
           w_pool_grp, pool_scale, w_pool_br, w_mix_out, g_mix_post, g_mem_pre, g_mem_kv,
           w_mem_q, w_mem_kv, w_mem_o, g_mem_post, g_ffn_pre, w_ffn_in, conv_w, conv_b,
           w_ffn_out, g_ffn_post):
    batch, seq, d = x.shape
    depth = w_in.shape[0]
    assert (batch, seq, d) == (1, SEQ, D_MODEL)
    inv_freq = ROPE_THETA ** (-np.arange(0, ROPE, 2, dtype=np.float32) / ROPE)
    freq = jnp.asarray(np.tile(inv_freq.astype(np.float32), 4)[None, :])
    pos = positions.reshape(seq, 1).astype(F32)
    xs = x.reshape(seq, d)
    mems = mem.reshape(N_MEM, d)
    for l in range(depth):
        z_gates, z_pool, z_lat = _in_proj(xs, g_mix_pre[l][None], *_prep_w_in(w_in[l]))
        q, kn, kpe, v = _mla_proj(z_lat, pos, freq, g_q_a[l][None], g_kv_a[l][None],
                                  _prep_w_q(w_q_b[l]), _prep_w_kv(w_kv_b[l]))
        o = _flash(q, kn, kpe, v)
        x1 = _mix(o, z_gates, z_pool, xs, w_attn_br[l].astype(BF16),
                  w_pool_grp[l].astype(BF16),
                  pool_scale[l][None], w_pool_br[l].astype(BF16),
                  w_mix_out[l].astype(BF16), g_mix_post[l][None])
        kv = _mem_kv(mems, g_mem_kv[l][None], w_mem_kv[l].astype(BF16))
        x2 = _mem_attn(x1, g_mem_pre[l][None], w_mem_q[l].astype(BF16), kv,
                       w_mem_o[l].astype(BF16), g_mem_post[l][None])
        xs = _ffn(x2, g_ffn_pre[l][None], w_ffn_in[l].astype(BF16), conv_w[l],
                  conv_b[l][None], w_ffn_out[l].astype(BF16), g_ffn_post[l][None])
    return xs.reshape(batch, seq, d)
```

```python
import functools
import math

import numpy as np
import jax
import jax.numpy as jnp
from jax import lax
from jax.experimental import pallas as pl
from jax.experimental.pallas import tpu as pltpu

F32 = jnp.float32
BF16 = jnp.bfloat16

D_MODEL = 2048
SEQ = 16384
N_HEADS = 16
Q_LORA = 512
KV_LORA = 512
NOPE = 128
ROPE = 64
V_DIM = 128
ROPE_THETA = 10000.0
ATTN_WIDTH = N_HEADS * V_DIM
POOL_WIDTH = 1024
POOL_WINDOWS = (2, 4, 8, 16)
POOL_GROUP_DIM = 256
N_MEM = 256
MEM_HEADS = 4
MEM_HEAD_DIM = 512
D_FF = 5632
EPS = 1e-6

LAT_Q = 0
LAT_KV = 512
LAT_KPE = 1024
LAT_WIDTH = 1280

HEAD_PAD = 256
HALO = 16
MIB = 1024 * 1024

CALLS = {
    "in_proj": dict(tm=1024, tn=1024, vmem_mib=58),
    "mla_proj": dict(tm=512, vmem_mib=48),
    "flash": dict(tq=512, tk=1024, rb=8, group=4, lag=1, vmem_mib=58),
    "mix": dict(tm=256, vmem_mib=56),
    "mem_kv": dict(tn=1024, vmem_mib=32),
    "mem_attn": dict(tm=512, vmem_mib=48),
    "ffn": dict(tm=512, tn=512, tiles_per_step=2, vmem_mib=58),
}


def _cparams(sem, vmem_mib):
    return pltpu.CompilerParams(dimension_semantics=sem,
                                vmem_limit_bytes=int(vmem_mib * MIB))


def _rms(xf, g):
    ms = jnp.mean(xf * xf, axis=-1, keepdims=True)
    return xf * lax.rsqrt(ms + EPS) * g


def _norm_rows(dst_ref, dst_off, src_ref, g, rows, chunk):
    def body(c, carry):
        r = pl.multiple_of(c * chunk, chunk)
        y = _rms(src_ref[pl.ds(r, chunk), :].astype(F32), g)
        dst_ref[pl.ds(dst_off + r, chunk), :] = y.astype(dst_ref.dtype)
        return carry
    lax.fori_loop(0, rows // chunk, body, 0)


def _in_proj_kernel(x_ref, g_ref, wg_ref, wp_ref, wl_ref, zg_ref, zp_ref, zl_ref, h_ref,
                    *, tm, gate_tiles):
    j = pl.program_id(1)

    @pl.when(j == 0)
    def _():
        _norm_rows(h_ref, 0, x_ref, g_ref[...], tm, 128)

    def project(w_ref, z_ref):
        z_ref[...] = jnp.dot(h_ref[...], w_ref[...],
                             preferred_element_type=F32).astype(z_ref.dtype)

    pl.when(j < gate_tiles)(lambda: project(wg_ref, zg_ref))
    pl.when(j == gate_tiles)(lambda: project(wp_ref, zp_ref))
    pl.when(j == gate_tiles + 1)(lambda: project(wl_ref, zl_ref))


def _in_proj(x, g, w_gates, w_pool, w_lat, *, tm, tn, vmem_mib):
    s, d = x.shape
    gate_tiles = w_gates.shape[1] // tn
    n_pool, n_lat = w_pool.shape[1], w_lat.shape[1]

    def gate_tile(j):
        return jnp.minimum(j, gate_tiles - 1)
    return pl.pallas_call(
        functools.partial(_in_proj_kernel, tm=tm, gate_tiles=gate_tiles),
        grid=(s // tm, gate_tiles + 2),
        in_specs=[pl.BlockSpec((tm, d), lambda i, j: (i, 0)),
                  pl.BlockSpec((1, d), lambda i, j: (0, 0)),
                  pl.BlockSpec((d, tn), lambda i, j: (0, gate_tile(j))),
                  pl.BlockSpec((d, n_pool), lambda i, j: (0, 0), pipeline_mode=pl.Buffered(1)),
                  pl.BlockSpec((d, n_lat), lambda i, j: (0, 0), pipeline_mode=pl.Buffered(1))],
        out_specs=[pl.BlockSpec((tm, tn), lambda i, j: (i, gate_tile(j))),
                   pl.BlockSpec((tm, n_pool), lambda i, j: (i, 0)),
                   pl.BlockSpec((tm, n_lat), lambda i, j: (i, 0))],
        out_shape=[jax.ShapeDtypeStruct((s, w_gates.shape[1]), BF16),
                   jax.ShapeDtypeStruct((s, n_pool), BF16),
                   jax.ShapeDtypeStruct((s, n_lat), BF16)],
        scratch_shapes=[pltpu.VMEM((tm, d), BF16)],
        compiler_params=_cparams(("parallel", "arbitrary"), vmem_mib),
        name="in_proj",
    )(x, g, w_gates, w_pool, w_lat)


def _mla_proj_kernel(ql_ref, ckv_ref, kp_ref, pos_ref, freq_ref, gq_ref, gkv_ref,
                     wq_ref, wkv_ref, q_ref, kn_ref, kpe_ref, v_ref, *, q_scale):
    qn = _rms(ql_ref[...].astype(F32), gq_ref[...]).astype(BF16)
    cn = _rms(ckv_ref[...].astype(F32), gkv_ref[...]).astype(BF16)
    ang = pos_ref[...] * freq_ref[...]
    lane = lax.broadcasted_iota(jnp.int32, (1, 2 * ROPE), 1)
    sign = jnp.where((lane >= ROPE) & (lane < ROPE + ROPE // 2), -1.0, 1.0)
    table = jnp.where(lane < ROPE, jnp.cos(ang), jnp.sin(ang) * sign)
    low = lane < ROPE

    def rot(t):
        t = t * table
        return t + pltpu.roll(t, ROPE, axis=1)

    kpe_ref[...] = jnp.where(low, rot(kp_ref[...].astype(F32)), 0.0).astype(BF16)
    for h in range(N_HEADS):
        qh = jnp.dot(qn, wq_ref[h], preferred_element_type=F32)
        q_ref[h, :, 0:NOPE] = (qh[:, 0:NOPE] * q_scale).astype(BF16)
        q_ref[h, :, NOPE:HEAD_PAD] = (rot(qh[:, NOPE:HEAD_PAD]) * q_scale).astype(BF16)
        kvh = jnp.dot(cn, wkv_ref[h], preferred_element_type=F32)
        kn_ref[h] = kvh[:, 0:NOPE].astype(BF16)
        v_ref[h] = kvh[:, NOPE:HEAD_PAD].astype(BF16)


def _mla_proj(z_lat, pos, freq, gq, gkv, wq, wkv, *, tm, vmem_mib):
    s = z_lat.shape[0]
    q_scale = (NOPE + ROPE) ** -0.5 * math.log2(math.e)

    def head_out(width):
        return jax.ShapeDtypeStruct((N_HEADS, s, width), BF16)

    def head_spec(width):
        return pl.BlockSpec((N_HEADS, tm, width), lambda i: (0, i, 0))
    const3 = lambda i: (0, 0, 0)
    return pl.pallas_call(
        functools.partial(_mla_proj_kernel, q_scale=q_scale),
        grid=(s // tm,),
        in_specs=[pl.BlockSpec((tm, Q_LORA), lambda i: (i, LAT_Q // Q_LORA)),
                  pl.BlockSpec((tm, KV_LORA), lambda i: (i, LAT_KV // KV_LORA)),
                  pl.BlockSpec((tm, 2 * ROPE), lambda i: (i, LAT_KPE // (2 * ROPE))),
                  pl.BlockSpec((tm, 1), lambda i: (i, 0)),
                  pl.BlockSpec((1, 2 * ROPE), lambda i: (0, 0)),
                  pl.BlockSpec((1, Q_LORA), lambda i: (0, 0)),
                  pl.BlockSpec((1, KV_LORA), lambda i: (0, 0)),
                  pl.BlockSpec((N_HEADS, Q_LORA, HEAD_PAD), const3,
                               pipeline_mode=pl.Buffered(1)),
                  pl.BlockSpec((N_HEADS, KV_LORA, HEAD_PAD), const3,
                               pipeline_mode=pl.Buffered(1))],
        out_specs=[head_spec(HEAD_PAD), head_spec(NOPE),
                   pl.BlockSpec((tm, HEAD_PAD - NOPE), lambda i: (i, 0)), head_spec(V_DIM)],
        out_shape=[head_out(HEAD_PAD), head_out(NOPE),
                   jax.ShapeDtypeStruct((s, HEAD_PAD - NOPE), BF16), head_out(V_DIM)],
        compiler_params=_cparams(("parallel",), vmem_mib),
        name="mla_proj",
    )(z_lat, z_lat, z_lat, pos, freq, gq, gkv, wq, wkv)


M_INIT = float(np.finfo(np.float32).min)
EDGE_BLOCK = 3


def _flash_kernel(q_ref, kn_ref, kpe_ref, v_ref, o_ref, *scratch, tq, tk, rb, group, lag):
    n_s = 2 * lag
    s_bufs, (p0, p1, a0, a1, m_ref, acc_ref) = scratch[:n_s], scratch[n_s:]
    p_bufs, a_bufs = (p0, p1), (a0, a1)
    r = tk // tq
    nb = rb // r
    band = pl.program_id(1)
    row0 = band * rb
    row_end = row0 + rb
    n_plain = r * nb * nb * band + r * (nb * (nb - 1) // 2)
    p1[...] = jnp.zeros(p1.shape, BF16)
    a1[...] = jnp.ones(a1.shape, F32)
    acc_ref[...] = jnp.zeros(acc_ref.shape, F32)
    m_ref[...] = jnp.full(m_ref.shape, M_INIT, F32)

    def last_tile(i):
        return lax.div(i, jnp.int32(r))

    def stage_a(e, row, j, width=tk):
        q = q_ref[0, pl.ds(pl.multiple_of(row * tq, tq), tq), :]
        cols = pl.ds(pl.multiple_of(j * tk, tk), width)
        kt = jnp.concatenate([kn_ref[0, cols, :], kpe_ref[cols, :]], axis=1)
        s_bufs[e % n_s][:, 0:width] = lax.dot_general(q, kt, (((1,), (1,)), ((), ())),
                                                      preferred_element_type=F32)

    def stage_b(e, row, width=tk, off=None):
        s = s_bufs[e % n_s][:, 0:width]
        if off is not None:
            rel = (lax.broadcasted_iota(jnp.int32, (tq, width), 1)
                   - lax.broadcasted_iota(jnp.int32, (tq, width), 0))
            s = jnp.where(rel <= off, s, -jnp.inf)
        m_prev = m_ref[row]
        m_new = jnp.maximum(m_prev, jnp.max(s, axis=-1, keepdims=True))
        a_bufs[e % 2][...] = jnp.exp2(m_prev - m_new)
        p_bufs[e % 2][:, 0:width] = jnp.exp2(s - m_new).astype(BF16)
        m_ref[row] = m_new

    def stage_c(e, row, j, width=tk):
        vt = jnp.concatenate([v_ref[0, pl.ds(pl.multiple_of(j * tk, tk), width), :],
                              jnp.ones((width, HEAD_PAD - V_DIM), BF16)], axis=1)
        acc = a_bufs[e % 2][...] * acc_ref[row] + jnp.dot(
            p_bufs[e % 2][:, 0:width], vt, preferred_element_type=F32)
        acc_ref[row] = acc
        return acc

    def advance(i, j):
        lt = last_tile(i)
        same_row = j + 1 < lt
        more_rows = i + 1 < row_end
        i_plain = jnp.where(same_row, i, jnp.where(more_rows, i + 1, row0))
        j_plain = jnp.where(same_row, j + 1, jnp.where(more_rows, 0, last_tile(row0)))
        on_edge = j == lt
        return (jnp.where(on_edge, i + 1, i_plain),
                jnp.where(on_edge, last_tile(i + 1), j_plain))

    first = jnp.maximum(row0, r)
    pairs = [(first, jnp.int32(0))]
    for _ in range(lag + 1):
        pairs.append(advance(*pairs[-1]))
    for e in range(lag):
        stage_a(e, pairs[e][0] - row0, pairs[e][1])

    def plain_group(_, state):
        state = list(state)
        for g in range(group):
            (ia, ja), (ib, jb), (ic, jc) = state[0], state[lag], state[lag + 1]
            stage_a(lag + g, ia - row0, ja)
            stage_b(g, ib - row0)
            stage_c(g - 1, ic - row0, jc)
            state = [advance(ia, ja)] + state[:-1]
        return tuple(state)

    state = tuple(pairs[lag - d] for d in range(lag + 1)) + (pairs[0],)
    state = lax.fori_loop(0, n_plain // group, plain_group, state)

    ic0, jc0 = state[lag + 1]
    jb0 = last_tile(row0)

    def edge_tick(k):
        if lag <= k + lag < rb:
            ka = k + lag
            stage_a(ka, ka, jb0 + ka // r, (ka % r + 1) * tq)
        if k < rb:
            stage_b(k, k, (k % r + 1) * tq, (k % r) * tq)
        if k == 0:
            stage_c(-1, ic0 - row0, jc0)
        else:
            kc = k - 1
            acc = stage_c(kc, kc, jb0 + kc // r, (kc % r + 1) * tq)
            o_ref[kc * tq:k * tq, :] = (
                acc[:, 0:V_DIM] / acc[:, V_DIM:HEAD_PAD]).astype(o_ref.dtype)

    for k0 in range(0, rb + 1, EDGE_BLOCK):
        @pl.when(band >= 0)
        def _():
            for k in range(k0, min(k0 + EDGE_BLOCK, rb + 1)):
                edge_tick(k)


def _flash(q, kn, kpe, v, *, tq, tk, rb, group, lag, vmem_mib):
    h, s, _ = q.shape
    band = tq * rb
    r = tk // tq
    nb = rb // r
    assert tk % tq == 0 and rb % r == 0 and s % band == 0
    assert group % (2 * lag) == 0 and nb >= 2 and lag <= rb
    assert (r * nb * nb) % group == 0 and (r * (nb * (nb - 1) // 2)) % group == 0
    return pl.pallas_call(
        functools.partial(_flash_kernel, tq=tq, tk=tk, rb=rb, group=group, lag=lag),
        grid=(h, s // band),
        in_specs=[pl.BlockSpec((1, band, HEAD_PAD), lambda hh, b: (hh, b, 0)),
                  pl.BlockSpec((1, s, NOPE), lambda hh, b: (hh, 0, 0)),
                  pl.BlockSpec((s, HEAD_PAD - NOPE), lambda hh, b: (0, 0),
                               pipeline_mode=pl.Buffered(1)),
                  pl.BlockSpec((1, s, V_DIM), lambda hh, b: (hh, 0, 0))],
        out_specs=pl.BlockSpec((band, V_DIM), lambda hh, b: (b, hh)),
        out_shape=jax.ShapeDtypeStruct((s, h * V_DIM), BF16),
        scratch_shapes=[pltpu.VMEM((tq, tk), F32)] * (2 * lag) + [
                        pltpu.VMEM((tq, tk), BF16), pltpu.VMEM((tq, tk), BF16),
                        pltpu.VMEM((tq, 1), F32), pltpu.VMEM((tq, 1), F32),
                        pltpu.VMEM((rb, tq, 1), F32),
                        pltpu.VMEM((rb, tq, HEAD_PAD), F32)],
        compiler_params=_cparams(("parallel", "arbitrary"), vmem_mib),
        name="flash",
    )(q, kn, kpe, v)


def _mix_kernel(o_ref, ga_ref, gp_ref, u_ref, uh_ref, x_ref, wab_ref, wpg_ref, ps_ref,
                wpb_ref, wmo_ref, g_ref, out_ref, ext_ref, yp_ref, *, tm):
    i = pl.program_id(0)
    ya = jnp.dot(o_ref[...], wab_ref[...], preferred_element_type=F32)
    ext_ref[0:HALO, :] = jnp.where(i == 0, 0.0, uh_ref[...].astype(F32))
    ext_ref[HALO:HALO + tm, :] = u_ref[...].astype(F32)
    t = i * tm + lax.broadcasted_iota(jnp.int32, (tm, 1), 0)
    for g, w in enumerate(POOL_WINDOWS):
        cols = slice(g * POOL_GROUP_DIM, (g + 1) * POOL_GROUP_DIM)
        cur = ext_ref[HALO:HALO + tm, cols]
        win = cur
        for b in range(1, w):
            win = win + ext_ref[HALO - b:HALO - b + tm, cols]
        cnt = jnp.minimum(t + 1, w).astype(F32)
        d = win / cnt - cur
        yg = jnp.dot(d.astype(BF16), wpg_ref[g], preferred_element_type=F32)
        yp_ref[:, cols] = (yg * ps_ref[:, cols]).astype(BF16)
    yp = jnp.dot(yp_ref[...], wpb_ref[...], preferred_element_type=F32)
    merged = (jax.nn.sigmoid(ga_ref[...].astype(F32)) * ya
              + jax.nn.sigmoid(gp_ref[...].astype(F32)) * yp)
    mo = jnp.dot(merged.astype(BF16), wmo_ref[...], preferred_element_type=F32)
    out_ref[...] = x_ref[...] + _rms(mo, g_ref[...])


def _resident(shape):
    zeros = (0,) * len(shape)
    return pl.BlockSpec(shape, lambda i: zeros, pipeline_mode=pl.Buffered(1))


def _mix(o, z_gates, z_pool, x, wab, wpg, ps, wpb, wmo, g, *, tm, vmem_mib):
    s, d = x.shape
    return pl.pallas_call(
        functools.partial(_mix_kernel, tm=tm),
        grid=(s // tm,),
        in_specs=[pl.BlockSpec((tm, ATTN_WIDTH), lambda i: (i, 0)),
                  pl.BlockSpec((tm, d), lambda i: (i, 0)),
                  pl.BlockSpec((tm, d), lambda i: (i, 1)),
                  pl.BlockSpec((tm, POOL_WIDTH), lambda i: (i, 0)),
                  pl.BlockSpec((HALO, POOL_WIDTH),
                               lambda i: (jnp.maximum(i * (tm // HALO) - 1, 0), 0)),
                  pl.BlockSpec((tm, d), lambda i: (i, 0)),
                  _resident(wab.shape), _resident(wpg.shape), _resident(ps.shape),
                  _resident(wpb.shape), _resident(wmo.shape), _resident(g.shape)],
        out_specs=pl.BlockSpec((tm, d), lambda i: (i, 0)),
        out_shape=jax.ShapeDtypeStruct((s, d), F32),
        scratch_shapes=[pltpu.VMEM((HALO + tm, POOL_WIDTH), F32),
                        pltpu.VMEM((tm, POOL_WIDTH), BF16)],
        compiler_params=_cparams(("parallel",), vmem_mib),
        name="mix",
    )(o, z_gates, z_gates, z_pool, z_pool, x, wab, wpg, ps, wpb, wmo, g)


def _mem_kv_kernel(mem_ref, g_ref, w_ref, kv_ref):
    mn = _rms(mem_ref[...], g_ref[...]).astype(BF16)
    kv_ref[...] = jnp.dot(mn, w_ref[...], preferred_element_type=F32).astype(kv_ref.dtype)


def _mem_kv(mem, g, w, *, tn, vmem_mib):
    m, d = mem.shape
    n = w.shape[1]
    return pl.pallas_call(
        _mem_kv_kernel,
        grid=(n // tn,),
        in_specs=[pl.BlockSpec((m, d), lambda j: (0, 0)),
                  pl.BlockSpec((1, d), lambda j: (0, 0)),
                  pl.BlockSpec((d, tn), lambda j: (0, j))],
        out_specs=pl.BlockSpec((m, tn), lambda j: (0, j)),
        out_shape=jax.ShapeDtypeStruct((m, n), BF16),
        compiler_params=_cparams(("parallel",), vmem_mib),
        name="mem_kv",
    )(mem, g, w)


def _mem_attn_kernel(x_ref, gpre_ref, wq_ref, kv_ref, wo_ref, gpost_ref, out_ref,
                     o_ref):
    x = x_ref[...]
    h = _rms(x, gpre_ref[...]).astype(BF16)
    scale = MEM_HEAD_DIM ** -0.5
    q = (jnp.dot(h, wq_ref[...], preferred_element_type=F32) * scale).astype(BF16)
    for hd in range(MEM_HEADS):
        cols = slice(hd * MEM_HEAD_DIM, (hd + 1) * MEM_HEAD_DIM)
        vcols = slice(D_MODEL + hd * MEM_HEAD_DIM, D_MODEL + (hd + 1) * MEM_HEAD_DIM)
        s = lax.dot_general(q[:, cols], kv_ref[:, cols], (((1,), (1,)), ((), ())),
                            preferred_element_type=F32)
        p = jnp.exp(s - jnp.max(s, axis=-1, keepdims=True))
        l = jnp.sum(p, axis=-1, keepdims=True)
        oh = jnp.dot(p.astype(BF16), kv_ref[:, vcols], preferred_element_type=F32)
        o_ref[:, cols] = (oh / l).astype(BF16)
    y = jnp.dot(o_ref[...], wo_ref[...], preferred_element_type=F32)
    out_ref[...] = x + _rms(y, gpost_ref[...])


def _mem_attn(x, gpre, wq, kv, wo, gpost, *, tm, vmem_mib):
    s, d = x.shape
    return pl.pallas_call(
        _mem_attn_kernel,
        grid=(s // tm,),
        in_specs=[pl.BlockSpec((tm, d), lambda i: (i, 0)),
                  _resident(gpre.shape), _resident(wq.shape), _resident(kv.shape),
                  _resident(wo.shape), _resident(gpost.shape)],
        out_specs=pl.BlockSpec((tm, d), lambda i: (i, 0)),
        out_shape=jax.ShapeDtypeStruct((s, d), F32),
        scratch_shapes=[pltpu.VMEM((tm, d), BF16)],
        compiler_params=_cparams(("parallel",), vmem_mib),
        name="mem_attn",
    )(x, gpre, wq, kv, wo, gpost)


def _gelu_tanh(x):
    c = math.sqrt(2.0 / math.pi)
    return 0.5 * x * (1.0 + jnp.tanh(c * (x + 0.044715 * (x * x * x))))


FFN_TILE_REFS = 7


def _ffn_kernel(x_ref, xh_ref, gpre_ref, *refs, tm, tiles_per_step, n_tiles):
    n_in = FFN_TILE_REFS * tiles_per_step
    tile_refs = [refs[FFN_TILE_REFS * t:FFN_TILE_REFS * (t + 1)] for t in range(tiles_per_step)]
    gpost_ref, out_ref, h_ref = refs[n_in:n_in + 3]
    a_refs = refs[n_in + 3:]
    i = pl.program_id(0)
    j = pl.program_id(1)

    @pl.when(j == 0)
    def _():
        hh = _rms(xh_ref[...], gpre_ref[...])
        h_ref[0:HALO, :] = jnp.where(i == 0, 0.0, hh).astype(BF16)
        _norm_rows(h_ref, HALO, x_ref, gpre_ref[...], tm, 128)
        out_ref[...] = jnp.zeros(out_ref.shape, F32)

    def conv(a_ref, w_ref, cw_ref, cb_ref):
        a_ref[...] = jnp.dot(h_ref[...], w_ref[...], preferred_element_type=F32)
        return (cb_ref[...]
                + a_ref[HALO - 2:HALO - 2 + tm, :] * cw_ref[0:1, :]
                + a_ref[HALO - 1:HALO - 1 + tm, :] * cw_ref[1:2, :]
                + a_ref[HALO:HALO + tm, :] * cw_ref[2:3, :])

    def tile_update(t):
        wg_ref, wu_ref, cwg_ref, cwu_ref, cbg_ref, cbu_ref, wo_ref = tile_refs[t]
        act = _gelu_tanh(conv(a_refs[t], wg_ref, cwg_ref, cbg_ref))
        up = conv(a_refs[t], wu_ref, cwu_ref, cbu_ref)
        return jnp.dot((act * up).astype(BF16), wo_ref[...], preferred_element_type=F32)

    def step(n):
        upd = tile_update(0)
        for t in range(1, n):
            upd = upd + tile_update(t)
        out_ref[...] += upd

    full_steps, rest = divmod(n_tiles, tiles_per_step)
    pl.when(j < full_steps)(lambda: step(tiles_per_step))
    if rest:
        pl.when(j == full_steps)(lambda: step(rest))

    @pl.when(j == pl.num_programs(1) - 1)
    def _():
        out_ref[...] = x_ref[...] + _rms(out_ref[...], gpost_ref[...])


def _ffn(x, gpre, w_in, conv_w, conv_b, w_out, gpost, *, tm, tn, tiles_per_step, vmem_mib):
    s, d = x.shape
    n_tiles = D_FF // tn

    def tile_specs(t):
        def tile(j):
            return jnp.minimum(j * tiles_per_step + t, n_tiles - 1)
        return [pl.BlockSpec((d, tn), lambda i, j: (0, tile(j))),
                pl.BlockSpec((d, tn), lambda i, j: (0, n_tiles + tile(j))),
                pl.BlockSpec((3, tn), lambda i, j: (0, tile(j))),
                pl.BlockSpec((3, tn), lambda i, j: (0, n_tiles + tile(j))),
                pl.BlockSpec((1, tn), lambda i, j: (0, tile(j))),
                pl.BlockSpec((1, tn), lambda i, j: (0, n_tiles + tile(j))),
                pl.BlockSpec((tn, d), lambda i, j: (tile(j), 0))]
    tile_args = (w_in, w_in, conv_w, conv_w, conv_b, conv_b, w_out)
    return pl.pallas_call(
        functools.partial(_ffn_kernel, tm=tm, tiles_per_step=tiles_per_step,
                          n_tiles=n_tiles),
        grid=(s // tm, pl.cdiv(n_tiles, tiles_per_step)),
        in_specs=[pl.BlockSpec((tm, d), lambda i, j: (i, 0)),
                  pl.BlockSpec((HALO, d),
                               lambda i, j: (jnp.maximum(i * (tm // HALO) - 1, 0), 0)),
                  pl.BlockSpec((1, d), lambda i, j: (0, 0))]
        + [spec for t in range(tiles_per_step) for spec in tile_specs(t)]
        + [pl.BlockSpec((1, d), lambda i, j: (0, 0))],
        out_specs=pl.BlockSpec((tm, d), lambda i, j: (i, 0)),
        out_shape=jax.ShapeDtypeStruct((s, d), F32),
        scratch_shapes=[pltpu.VMEM((HALO + tm, d), BF16)]
        + [pltpu.VMEM((HALO + tm, tn), F32)] * tiles_per_step,
        compiler_params=_cparams(("parallel", "arbitrary"), vmem_mib),
        name="ffn",
    )(x, x, gpre, *(tile_args * tiles_per_step), gpost)


def _swap_halves(w):
    half = ROPE // 2
    return jnp.concatenate([w[..., half:], w[..., :half]], axis=-1)


def _prep_w_in(w_in):
    lat_end = Q_LORA + KV_LORA + ROPE
    w = w_in.astype(BF16)
    lat, k_pe = w[:, :lat_end], w[:, Q_LORA + KV_LORA:lat_end]
    pad = jnp.zeros((D_MODEL, LAT_WIDTH - LAT_KPE - 2 * ROPE), BF16)
    w_lat = jnp.concatenate([lat, _swap_halves(k_pe), pad], axis=1)
    return w[:, lat_end + POOL_WIDTH:], w[:, lat_end:lat_end + POOL_WIDTH], w_lat


def _prep_w_q(w_q_b):
    w = w_q_b.reshape(Q_LORA, N_HEADS, NOPE + ROPE).transpose(1, 0, 2)
    pe = w[..., NOPE:]
    return jnp.concatenate([w, _swap_halves(pe)], axis=-1).astype(BF16)


def _prep_w_kv(w_kv_b):
    return w_kv_b.reshape(KV_LORA, N_HEADS, NOPE + V_DIM).transpose(1, 0, 2).astype(BF16)


def kernel(x, mem, positions, g_mix_pre, w_in, g_q_a, w_q_b, g_kv_a, w_kv_b, w_attn_br,
           w_pool_grp, pool_scale, w_pool_br, w_mix_out, g_mix_post, g_mem_pre, g_mem_kv,
           w_mem_q, w_mem_kv, w_mem_o, g_mem_post, g_ffn_pre, w_ffn_in, conv_w, conv_b,
           w_ffn_out, g_ffn_post):
    batch, seq, d = x.shape
    depth = w_in.shape[0]
    assert (batch, seq, d) == (1, SEQ, D_MODEL)
    inv_freq = ROPE_THETA ** (-np.arange(0, ROPE, 2, dtype=np.float32) / ROPE)
    freq = jnp.asarray(np.tile(inv_freq.astype(np.float32), 4)[None, :])
    pos = positions.reshape(seq, 1).astype(F32)
    xs = x.reshape(seq, d)
    mems = mem.reshape(N_MEM, d)
    for l in range(depth):
        z_gates, z_pool, z_lat = _in_proj(xs, g_mix_pre[l][None], *_prep_w_in(w_in[l]),
                                          **CALLS["in_proj"])
        q, kn, kpe, v = _mla_proj(z_lat, pos, freq, g_q_a[l][None], g_kv_a[l][None],
                                  _prep_w_q(w_q_b[l]), _prep_w_kv(w_kv_b[l]),
                                  **CALLS["mla_proj"])
        o = _flash(q, kn, kpe, v, **CALLS["flash"])
        x1 = _mix(o, z_gates, z_pool, xs, w_attn_br[l].astype(BF16),
                  w_pool_grp[l].astype(BF16), pool_scale[l][None],
                  w_pool_br[l].astype(BF16), w_mix_out[l].astype(BF16),
                  g_mix_post[l][None], **CALLS["mix"])
        kv = _mem_kv(mems, g_mem_kv[l][None], w_mem_kv[l].astype(BF16), **CALLS["mem_kv"])
        x2 = _mem_attn(x1, g_mem_pre[l][None], w_mem_q[l].astype(BF16), kv,
                       w_mem_o[l].astype(BF16), g_mem_post[l][None], **CALLS["mem_attn"])
        xs = _ffn(x2, g_ffn_pre[l][None], w_ffn_in[l].astype(BF16), conv_w[l],
                  conv_b[l][None], w_ffn_out[l].astype(BF16), g_ffn_post[l][None],
                  **CALLS["ffn"])
    return xs.reshape(batch, seq, d)
```

```python
import functools
import math

import numpy as np
import jax
import jax.numpy as jnp
from jax import lax
from jax.experimental import pallas as pl
from jax.experimental.pallas import tpu as pltpu

F32 = jnp.float32
BF16 = jnp.bfloat16

D_MODEL = 2048
SEQ = 16384
N_HEADS = 16
Q_LORA = 512
KV_LORA = 512
NOPE = 128
ROPE = 64
V_DIM = 128
ROPE_THETA = 10000.0
ATTN_WIDTH = N_HEADS * V_DIM
POOL_WIDTH = 1024
POOL_WINDOWS = (2, 4, 8, 16)
POOL_GROUP_DIM = 256
N_MEM = 256
MEM_HEADS = 4
MEM_HEAD_DIM = 512
D_FF = 5632
EPS = 1e-6

LAT_Q = 0
LAT_KV = 512
LAT_KPE = 1024
LAT_WIDTH = 1280

HEAD_PAD = 256
HALO = 16
MIB = 1024 * 1024

CALLS = {
    "in_proj": dict(tm=1024, tn=1024, vmem_mib=58),
    "mla_proj": dict(tm=512, vmem_mib=48),
    "flash": dict(tq=512, tk=1024, rb=8, group=4, lag=1, vmem_mib=58),
    "mix": dict(tm=256, vmem_mib=56),
    "mem_kv": dict(tn=1024, vmem_mib=32),
    "mem_attn": dict(tm=512, vmem_mib=48),
    "ffn": dict(tm=512, tn=512, tiles_per_step=2, vmem_mib=58),
}


def _cparams(sem, vmem_mib):
    return pltpu.CompilerParams(dimension_semantics=sem,
                                vmem_limit_bytes=int(vmem_mib * MIB))


def _rms(xf, g):
    ms = jnp.mean(xf * xf, axis=-1, keepdims=True)
    return xf * lax.rsqrt(ms + EPS) * g


def _norm_rows(dst_ref, dst_off, src_ref, g, rows, chunk):
    def body(c, carry):
        r = pl.multiple_of(c * chunk, chunk)
        y = _rms(src_ref[pl.ds(r, chunk), :].astype(F32), g)
        dst_ref[pl.ds(dst_off + r, chunk), :] = y.astype(dst_ref.dtype)
        return carry
    lax.fori_loop(0, rows // chunk, body, 0)


def _with_casts(kernel, n_in, n_out, n_cast):
    def wrapped(*refs):
        ins, rest = refs[:n_in], refs[n_in:]
        cast_in, rest = rest[:n_cast], rest[n_cast:]
        outs, rest = rest[:n_out], rest[n_out:]
        cast_out, scratch = rest[:n_cast], rest[n_cast:]
        for src, dst in zip(cast_in, cast_out):
            dst[...] = src[...].astype(BF16)
        kernel(*ins, *outs, *scratch)
    return wrapped


def _cast_specs(weights, steps):
    specs, shapes = [], []
    for w in weights:
        rows = w.shape[0] // steps
        assert rows * steps == w.shape[0] and rows % 16 == 0
        specs.append(pl.BlockSpec((rows, w.shape[1]), lambda i: (i, 0)))
        shapes.append(jax.ShapeDtypeStruct(w.shape, BF16))
    return specs, shapes


def _in_proj_kernel(x_ref, g_ref, wg_ref, wp_ref, wl_ref, zg_ref, zp_ref, zl_ref, h_ref,
                    *, tm, gate_tiles):
    j = pl.program_id(1)

    @pl.when(j == 0)
    def _():
        _norm_rows(h_ref, 0, x_ref, g_ref[...], tm, 128)

    def project(w_ref, z_ref):
        z_ref[...] = jnp.dot(h_ref[...], w_ref[...],
                             preferred_element_type=F32).astype(z_ref.dtype)

    pl.when(j < gate_tiles)(lambda: project(wg_ref, zg_ref))
    pl.when(j == gate_tiles)(lambda: project(wp_ref, zp_ref))
    pl.when(j == gate_tiles + 1)(lambda: project(wl_ref, zl_ref))


def _in_proj(x, g, w_gates, w_pool, w_lat, *, tm, tn, vmem_mib):
    s, d = x.shape
    gate_tiles = w_gates.shape[1] // tn
    n_pool, n_lat = w_pool.shape[1], w_lat.shape[1]

    def gate_tile(j):
        return jnp.minimum(j, gate_tiles - 1)
    return pl.pallas_call(
        functools.partial(_in_proj_kernel, tm=tm, gate_tiles=gate_tiles),
        grid=(s // tm, gate_tiles + 2),
        in_specs=[pl.BlockSpec((tm, d), lambda i, j: (i, 0)),
                  pl.BlockSpec((1, d), lambda i, j: (0, 0)),
                  pl.BlockSpec((d, tn), lambda i, j: (0, gate_tile(j))),
                  pl.BlockSpec((d, n_pool), lambda i, j: (0, 0), pipeline_mode=pl.Buffered(1)),
                  pl.BlockSpec((d, n_lat), lambda i, j: (0, 0), pipeline_mode=pl.Buffered(1))],
        out_specs=[pl.BlockSpec((tm, tn), lambda i, j: (i, gate_tile(j))),
                   pl.BlockSpec((tm, n_pool), lambda i, j: (i, 0)),
                   pl.BlockSpec((tm, n_lat), lambda i, j: (i, 0))],
        out_shape=[jax.ShapeDtypeStruct((s, w_gates.shape[1]), BF16),
                   jax.ShapeDtypeStruct((s, n_pool), BF16),
                   jax.ShapeDtypeStruct((s, n_lat), BF16)],
        scratch_shapes=[pltpu.VMEM((tm, d), BF16)],
        compiler_params=_cparams(("parallel", "arbitrary"), vmem_mib),
        name="in_proj",
    )(x, g, w_gates, w_pool, w_lat)


def _mla_proj_kernel(ql_ref, ckv_ref, kp_ref, pos_ref, freq_ref, gq_ref, gkv_ref,
                     wq_ref, wkv_ref, q_ref, kn_ref, kpe_ref, v_ref, *, q_scale):
    qn = _rms(ql_ref[...].astype(F32), gq_ref[...]).astype(BF16)
    cn = _rms(ckv_ref[...].astype(F32), gkv_ref[...]).astype(BF16)
    ang = pos_ref[...] * freq_ref[...]
    lane = lax.broadcasted_iota(jnp.int32, (1, 2 * ROPE), 1)
    sign = jnp.where((lane >= ROPE) & (lane < ROPE + ROPE // 2), -1.0, 1.0)
    table = jnp.where(lane < ROPE, jnp.cos(ang), jnp.sin(ang) * sign)
    low = lane < ROPE

    def rot(t):
        t = t * table
        return t + pltpu.roll(t, ROPE, axis=1)

    kpe_ref[...] = jnp.where(low, rot(kp_ref[...].astype(F32)), 0.0).astype(BF16)
    for h in range(N_HEADS):
        qh = jnp.dot(qn, wq_ref[h], preferred_element_type=F32)
        q_ref[h, :, 0:NOPE] = (qh[:, 0:NOPE] * q_scale).astype(BF16)
        q_ref[h, :, NOPE:HEAD_PAD] = (rot(qh[:, NOPE:HEAD_PAD]) * q_scale).astype(BF16)
        kvh = jnp.dot(cn, wkv_ref[h], preferred_element_type=F32)
        kn_ref[h] = kvh[:, 0:NOPE].astype(BF16)
        v_ref[h] = kvh[:, NOPE:HEAD_PAD].astype(BF16)


def _mla_proj(z_lat, pos, freq, gq, gkv, wq, wkv, cast, *, tm, vmem_mib):
    s = z_lat.shape[0]
    q_scale = (NOPE + ROPE) ** -0.5 * math.log2(math.e)
    cast_specs, cast_shapes = _cast_specs(cast, s // tm)

    def head_out(width):
        return jax.ShapeDtypeStruct((N_HEADS, s, width), BF16)

    def head_spec(width):
        return pl.BlockSpec((N_HEADS, tm, width), lambda i: (0, i, 0))
    const3 = lambda i: (0, 0, 0)
    return pl.pallas_call(
        _with_casts(functools.partial(_mla_proj_kernel, q_scale=q_scale), 9, 4, len(cast)),
        grid=(s // tm,),
        in_specs=[pl.BlockSpec((tm, Q_LORA), lambda i: (i, LAT_Q // Q_LORA)),
                  pl.BlockSpec((tm, KV_LORA), lambda i: (i, LAT_KV // KV_LORA)),
                  pl.BlockSpec((tm, 2 * ROPE), lambda i: (i, LAT_KPE // (2 * ROPE))),
                  pl.BlockSpec((tm, 1), lambda i: (i, 0)),
                  pl.BlockSpec((1, 2 * ROPE), lambda i: (0, 0)),
                  pl.BlockSpec((1, Q_LORA), lambda i: (0, 0)),
                  pl.BlockSpec((1, KV_LORA), lambda i: (0, 0)),
                  pl.BlockSpec((N_HEADS, Q_LORA, HEAD_PAD), const3,
                               pipeline_mode=pl.Buffered(1)),
                  pl.BlockSpec((N_HEADS, KV_LORA, HEAD_PAD), const3,
                               pipeline_mode=pl.Buffered(1))] + cast_specs,
        out_specs=[head_spec(HEAD_PAD), head_spec(NOPE),
                   pl.BlockSpec((tm, HEAD_PAD - NOPE), lambda i: (i, 0)),
                   head_spec(V_DIM)] + cast_specs,
        out_shape=[head_out(HEAD_PAD), head_out(NOPE),
                   jax.ShapeDtypeStruct((s, HEAD_PAD - NOPE), BF16),
                   head_out(V_DIM)] + cast_shapes,
        compiler_params=_cparams(("parallel",), vmem_mib),
        name="mla_proj",
    )(z_lat, z_lat, z_lat, pos, freq, gq, gkv, wq, wkv, *cast)


M_INIT = float(np.finfo(np.float32).min)
EDGE_BLOCK = 3


def _flash_kernel(q_ref, kn_ref, kpe_ref, v_ref, o_ref, *scratch, tq, tk, rb, group, lag):
    n_s = 2 * lag
    s_bufs, (p0, p1, a0, a1, m_ref, acc_ref) = scratch[:n_s], scratch[n_s:]
    p_bufs, a_bufs = (p0, p1), (a0, a1)
    r = tk // tq
    nb = rb // r
    band = pl.program_id(1)
    row0 = band * rb
    row_end = row0 + rb
    n_plain = r * nb * nb * band + r * (nb * (nb - 1) // 2)
    p1[...] = jnp.zeros(p1.shape, BF16)
    a1[...] = jnp.ones(a1.shape, F32)
    acc_ref[...] = jnp.zeros(acc_ref.shape, F32)
    m_ref[...] = jnp.full(m_ref.shape, M_INIT, F32)

    def last_tile(i):
        return lax.div(i, jnp.int32(r))

    def stage_a(e, row, j, width=tk):
        q = q_ref[0, pl.ds(pl.multiple_of(row * tq, tq), tq), :]
        cols = pl.ds(pl.multiple_of(j * tk, tk), width)
        kt = jnp.concatenate([kn_ref[0, cols, :], kpe_ref[cols, :]], axis=1)
        s_bufs[e % n_s][:, 0:width] = lax.dot_general(q, kt, (((1,), (1,)), ((), ())),
                                                      preferred_element_type=F32)

    def stage_b(e, row, width=tk, off=None):
        s = s_bufs[e % n_s][:, 0:width]
        if off is not None:
            rel = (lax.broadcasted_iota(jnp.int32, (tq, width), 1)
                   - lax.broadcasted_iota(jnp.int32, (tq, width), 0))
            s = jnp.where(rel <= off, s, -jnp.inf)
        m_prev = m_ref[row]
        m_new = jnp.maximum(m_prev, jnp.max(s, axis=-1, keepdims=True))
        a_bufs[e % 2][...] = jnp.exp2(m_prev - m_new)
        p_bufs[e % 2][:, 0:width] = jnp.exp2(s - m_new).astype(BF16)
        m_ref[row] = m_new

    def stage_c(e, row, j, width=tk):
        vt = jnp.concatenate([v_ref[0, pl.ds(pl.multiple_of(j * tk, tk), width), :],
                              jnp.ones((width, HEAD_PAD - V_DIM), BF16)], axis=1)
        acc = a_bufs[e % 2][...] * acc_ref[row] + jnp.dot(
            p_bufs[e % 2][:, 0:width], vt, preferred_element_type=F32)
        acc_ref[row] = acc
        return acc

    def advance(i, j):
        lt = last_tile(i)
        same_row = j + 1 < lt
        more_rows = i + 1 < row_end
        i_plain = jnp.where(same_row, i, jnp.where(more_rows, i + 1, row0))
        j_plain = jnp.where(same_row, j + 1, jnp.where(more_rows, 0, last_tile(row0)))
        on_edge = j == lt
        return (jnp.where(on_edge, i + 1, i_plain),
                jnp.where(on_edge, last_tile(i + 1), j_plain))

    first = jnp.maximum(row0, r)
    pairs = [(first, jnp.int32(0))]
    for _ in range(lag + 1):
        pairs.append(advance(*pairs[-1]))
    for e in range(lag):
        stage_a(e, pairs[e][0] - row0, pairs[e][1])

    def plain_group(_, state):
        state = list(state)
        for g in range(group):
            (ia, ja), (ib, jb), (ic, jc) = state[0], state[lag], state[lag + 1]
            stage_a(lag + g, ia - row0, ja)
            stage_b(g, ib - row0)
            stage_c(g - 1, ic - row0, jc)
            state = [advance(ia, ja)] + state[:-1]
        return tuple(state)

    state = tuple(pairs[lag - d] for d in range(lag + 1)) + (pairs[0],)
    state = lax.fori_loop(0, n_plain // group, plain_group, state)

    ic0, jc0 = state[lag + 1]
    jb0 = last_tile(row0)

    def edge_tick(k):
        if lag <= k + lag < rb:
            ka = k + lag
            stage_a(ka, ka, jb0 + ka // r, (ka % r + 1) * tq)
        if k < rb:
            stage_b(k, k, (k % r + 1) * tq, (k % r) * tq)
        if k == 0:
            stage_c(-1, ic0 - row0, jc0)
        else:
            kc = k - 1
            acc = stage_c(kc, kc, jb0 + kc // r, (kc % r + 1) * tq)
            o_ref[kc * tq:k * tq, :] = (
                acc[:, 0:V_DIM] / acc[:, V_DIM:HEAD_PAD]).astype(o_ref.dtype)

    for k0 in range(0, rb + 1, EDGE_BLOCK):
        @pl.when(band >= 0)
        def _():
            for k in range(k0, min(k0 + EDGE_BLOCK, rb + 1)):
                edge_tick(k)


def _flash(q, kn, kpe, v, *, tq, tk, rb, group, lag, vmem_mib):
    h, s, _ = q.shape
    band = tq * rb
    r = tk // tq
    nb = rb // r
    assert tk % tq == 0 and rb % r == 0 and s % band == 0
    assert group % (2 * lag) == 0 and nb >= 2 and lag <= rb
    assert (r * nb * nb) % group == 0 and (r * (nb * (nb - 1) // 2)) % group == 0
    return pl.pallas_call(
        functools.partial(_flash_kernel, tq=tq, tk=tk, rb=rb, group=group, lag=lag),
        grid=(h, s // band),
        in_specs=[pl.BlockSpec((1, band, HEAD_PAD), lambda hh, b: (hh, b, 0)),
                  pl.BlockSpec((1, s, NOPE), lambda hh, b: (hh, 0, 0)),
                  pl.BlockSpec((s, HEAD_PAD - NOPE), lambda hh, b: (0, 0),
                               pipeline_mode=pl.Buffered(1)),
                  pl.BlockSpec((1, s, V_DIM), lambda hh, b: (hh, 0, 0))],
        out_specs=pl.BlockSpec((band, V_DIM), lambda hh, b: (b, hh)),
        out_shape=jax.ShapeDtypeStruct((s, h * V_DIM), BF16),
        scratch_shapes=[pltpu.VMEM((tq, tk), F32)] * (2 * lag) + [
                        pltpu.VMEM((tq, tk), BF16), pltpu.VMEM((tq, tk), BF16),
                        pltpu.VMEM((tq, 1), F32), pltpu.VMEM((tq, 1), F32),
                        pltpu.VMEM((rb, tq, 1), F32),
                        pltpu.VMEM((rb, tq, HEAD_PAD), F32)],
        compiler_params=_cparams(("parallel", "arbitrary"), vmem_mib),
        name="flash",
    )(q, kn, kpe, v)


def _mix_kernel(o_ref, ga_ref, gp_ref, u_ref, uh_ref, x_ref, wab_ref, wpg_ref, ps_ref,
                wpb_ref, wmo_ref, g_ref, out_ref, ext_ref, yp_ref, *, tm):
    i = pl.program_id(0)
    ya = jnp.dot(o_ref[...], wab_ref[...], preferred_element_type=F32)
    ext_ref[0:HALO, :] = jnp.where(i == 0, 0.0, uh_ref[...].astype(F32))
    ext_ref[HALO:HALO + tm, :] = u_ref[...].astype(F32)
    t = i * tm + lax.broadcasted_iota(jnp.int32, (tm, 1), 0)
    for g, w in enumerate(POOL_WINDOWS):
        cols = slice(g * POOL_GROUP_DIM, (g + 1) * POOL_GROUP_DIM)
        cur = ext_ref[HALO:HALO + tm, cols]
        win = cur
        for b in range(1, w):
            win = win + ext_ref[HALO - b:HALO - b + tm, cols]
        cnt = jnp.minimum(t + 1, w).astype(F32)
        d = win / cnt - cur
        yg = jnp.dot(d.astype(BF16), wpg_ref[g], preferred_element_type=F32)
        yp_ref[:, cols] = (yg * ps_ref[:, cols]).astype(BF16)
    yp = jnp.dot(yp_ref[...], wpb_ref[...], preferred_element_type=F32)
    merged = (jax.nn.sigmoid(ga_ref[...].astype(F32)) * ya
              + jax.nn.sigmoid(gp_ref[...].astype(F32)) * yp)
    mo = jnp.dot(merged.astype(BF16), wmo_ref[...], preferred_element_type=F32)
    out_ref[...] = x_ref[...] + _rms(mo, g_ref[...])


def _resident(shape):
    zeros = (0,) * len(shape)
    return pl.BlockSpec(shape, lambda i: zeros, pipeline_mode=pl.Buffered(1))


def _mix(o, z_gates, z_pool, x, wab, wpg, ps, wpb, wmo, g, cast, *, tm, vmem_mib):
    s, d = x.shape
    cast_specs, cast_shapes = _cast_specs(cast, s // tm)
    return pl.pallas_call(
        _with_casts(functools.partial(_mix_kernel, tm=tm), 12, 1, len(cast)),
        grid=(s // tm,),
        in_specs=[pl.BlockSpec((tm, ATTN_WIDTH), lambda i: (i, 0)),
                  pl.BlockSpec((tm, d), lambda i: (i, 0)),
                  pl.BlockSpec((tm, d), lambda i: (i, 1)),
                  pl.BlockSpec((tm, POOL_WIDTH), lambda i: (i, 0)),
                  pl.BlockSpec((HALO, POOL_WIDTH),
                               lambda i: (jnp.maximum(i * (tm // HALO) - 1, 0), 0)),
                  pl.BlockSpec((tm, d), lambda i: (i, 0)),
                  _resident(wab.shape), _resident(wpg.shape), _resident(ps.shape),
                  _resident(wpb.shape), _resident(wmo.shape), _resident(g.shape)]
        + cast_specs,
        out_specs=[pl.BlockSpec((tm, d), lambda i: (i, 0))] + cast_specs,
        out_shape=[jax.ShapeDtypeStruct((s, d), F32)] + cast_shapes,
        scratch_shapes=[pltpu.VMEM((HALO + tm, POOL_WIDTH), F32),
                        pltpu.VMEM((tm, POOL_WIDTH), BF16)],
        compiler_params=_cparams(("parallel",), vmem_mib),
        name="mix",
    )(o, z_gates, z_gates, z_pool, z_pool, x, wab, wpg, ps, wpb, wmo, g, *cast)


def _mem_kv_kernel(mem_ref, g_ref, w_ref, kv_ref):
    mn = _rms(mem_ref[...], g_ref[...]).astype(BF16)
    kv_ref[...] = jnp.dot(mn, w_ref[...], preferred_element_type=F32).astype(kv_ref.dtype)


def _mem_kv(mem, g, w, *, tn, vmem_mib):
    m, d = mem.shape
    n = w.shape[1]
    return pl.pallas_call(
        _mem_kv_kernel,
        grid=(n // tn,),
        in_specs=[pl.BlockSpec((m, d), lambda j: (0, 0)),
                  pl.BlockSpec((1, d), lambda j: (0, 0)),
                  pl.BlockSpec((d, tn), lambda j: (0, j))],
        out_specs=pl.BlockSpec((m, tn), lambda j: (0, j)),
        out_shape=jax.ShapeDtypeStruct((m, n), BF16),
        compiler_params=_cparams(("parallel",), vmem_mib),
        name="mem_kv",
    )(mem, g, w)


def _mem_attn_kernel(x_ref, gpre_ref, wq_ref, kv_ref, wo_ref, gpost_ref, out_ref,
                     o_ref):
    x = x_ref[...]
    h = _rms(x, gpre_ref[...]).astype(BF16)
    scale = MEM_HEAD_DIM ** -0.5
    q = (jnp.dot(h, wq_ref[...], preferred_element_type=F32) * scale).astype(BF16)
    for hd in range(MEM_HEADS):
        cols = slice(hd * MEM_HEAD_DIM, (hd + 1) * MEM_HEAD_DIM)
        vcols = slice(D_MODEL + hd * MEM_HEAD_DIM, D_MODEL + (hd + 1) * MEM_HEAD_DIM)
        s = lax.dot_general(q[:, cols], kv_ref[:, cols], (((1,), (1,)), ((), ())),
                            preferred_element_type=F32)
        p = jnp.exp(s - jnp.max(s, axis=-1, keepdims=True))
        l = jnp.sum(p, axis=-1, keepdims=True)
        oh = jnp.dot(p.astype(BF16), kv_ref[:, vcols], preferred_element_type=F32)
        o_ref[:, cols] = (oh / l).astype(BF16)
    y = jnp.dot(o_ref[...], wo_ref[...], preferred_element_type=F32)
    out_ref[...] = x + _rms(y, gpost_ref[...])


def _mem_attn(x, gpre, wq, kv, wo, gpost, cast, *, tm, vmem_mib):
    s, d = x.shape
    cast_specs, cast_shapes = _cast_specs(cast, s // tm)
    return pl.pallas_call(
        _with_casts(_mem_attn_kernel, 6, 1, len(cast)),
        grid=(s // tm,),
        in_specs=[pl.BlockSpec((tm, d), lambda i: (i, 0)),
                  _resident(gpre.shape), _resident(wq.shape), _resident(kv.shape),
                  _resident(wo.shape), _resident(gpost.shape)] + cast_specs,
        out_specs=[pl.BlockSpec((tm, d), lambda i: (i, 0))] + cast_specs,
        out_shape=[jax.ShapeDtypeStruct((s, d), F32)] + cast_shapes,
        scratch_shapes=[pltpu.VMEM((tm, d), BF16)],
        compiler_params=_cparams(("parallel",), vmem_mib),
        name="mem_attn",
    )(x, gpre, wq, kv, wo, gpost, *cast)


def _gelu_tanh(x):
    c = math.sqrt(2.0 / math.pi)
    return 0.5 * x * (1.0 + jnp.tanh(c * (x + 0.044715 * (x * x * x))))


FFN_TILE_REFS = 7


def _ffn_kernel(x_ref, xh_ref, gpre_ref, *refs, tm, tiles_per_step, n_tiles):
    n_in = FFN_TILE_REFS * tiles_per_step
    tile_refs = [refs[FFN_TILE_REFS * t:FFN_TILE_REFS * (t + 1)] for t in range(tiles_per_step)]
    gpost_ref, out_ref, h_ref = refs[n_in:n_in + 3]
    a_refs = refs[n_in + 3:]
    i = pl.program_id(0)
    j = pl.program_id(1)

    @pl.when(j == 0)
    def _():
        hh = _rms(xh_ref[...], gpre_ref[...])
        h_ref[0:HALO, :] = jnp.where(i == 0, 0.0, hh).astype(BF16)
        _norm_rows(h_ref, HALO, x_ref, gpre_ref[...], tm, 128)
        out_ref[...] = jnp.zeros(out_ref.shape, F32)

    def conv(a_ref, w_ref, cw_ref, cb_ref):
        a_ref[...] = jnp.dot(h_ref[...], w_ref[...], preferred_element_type=F32)
        return (cb_ref[...]
                + a_ref[HALO - 2:HALO - 2 + tm, :] * cw_ref[0:1, :]
                + a_ref[HALO - 1:HALO - 1 + tm, :] * cw_ref[1:2, :]
                + a_ref[HALO:HALO + tm, :] * cw_ref[2:3, :])

    def tile_update(t):
        wg_ref, wu_ref, cwg_ref, cwu_ref, cbg_ref, cbu_ref, wo_ref = tile_refs[t]
        act = _gelu_tanh(conv(a_refs[t], wg_ref, cwg_ref, cbg_ref))
        up = conv(a_refs[t], wu_ref, cwu_ref, cbu_ref)
        return jnp.dot((act * up).astype(BF16), wo_ref[...], preferred_element_type=F32)

    def step(n):
        upd = tile_update(0)
        for t in range(1, n):
            upd = upd + tile_update(t)
        out_ref[...] += upd

    full_steps, rest = divmod(n_tiles, tiles_per_step)
    pl.when(j < full_steps)(lambda: step(tiles_per_step))
    if rest:
        pl.when(j == full_steps)(lambda: step(rest))

    @pl.when(j == pl.num_programs(1) - 1)
    def _():
        out_ref[...] = x_ref[...] + _rms(out_ref[...], gpost_ref[...])


def _ffn(x, gpre, w_in, conv_w, conv_b, w_out, gpost, *, tm, tn, tiles_per_step, vmem_mib):
    s, d = x.shape
    n_tiles = D_FF // tn

    def tile_specs(t):
        def tile(j):
            return jnp.minimum(j * tiles_per_step + t, n_tiles - 1)
        return [pl.BlockSpec((d, tn), lambda i, j: (0, tile(j))),
                pl.BlockSpec((d, tn), lambda i, j: (0, n_tiles + tile(j))),
                pl.BlockSpec((3, tn), lambda i, j: (0, tile(j))),
                pl.BlockSpec((3, tn), lambda i, j: (0, n_tiles + tile(j))),
                pl.BlockSpec((1, tn), lambda i, j: (0, tile(j))),
                pl.BlockSpec((1, tn), lambda i, j: (0, n_tiles + tile(j))),
                pl.BlockSpec((tn, d), lambda i, j: (tile(j), 0))]
    tile_args = (w_in, w_in, conv_w, conv_w, conv_b, conv_b, w_out)
    return pl.pallas_call(
        functools.partial(_ffn_kernel, tm=tm, tiles_per_step=tiles_per_step,
                          n_tiles=n_tiles),
        grid=(s // tm, pl.cdiv(n_tiles, tiles_per_step)),
        in_specs=[pl.BlockSpec((tm, d), lambda i, j: (i, 0)),
                  pl.BlockSpec((HALO, d),
                               lambda i, j: (jnp.maximum(i * (tm // HALO) - 1, 0), 0)),
                  pl.BlockSpec((1, d), lambda i, j: (0, 0))]
        + [spec for t in range(tiles_per_step) for spec in tile_specs(t)]
        + [pl.BlockSpec((1, d), lambda i, j: (0, 0))],
        out_specs=pl.BlockSpec((tm, d), lambda i, j: (i, 0)),
        out_shape=jax.ShapeDtypeStruct((s, d), F32),
        scratch_shapes=[pltpu.VMEM((HALO + tm, d), BF16)]
        + [pltpu.VMEM((HALO + tm, tn), F32)] * tiles_per_step,
        compiler_params=_cparams(("parallel", "arbitrary"), vmem_mib),
        name="ffn",
    )(x, x, gpre, *(tile_args * tiles_per_step), gpost)


def _swap_halves(w):
    half = ROPE // 2
    return jnp.concatenate([w[..., half:], w[..., :half]], axis=-1)


def _prep_w_in(w_in):
    lat_end = Q_LORA + KV_LORA + ROPE
    w = w_in.astype(BF16)
    lat, k_pe = w[:, :lat_end], w[:, Q_LORA + KV_LORA:lat_end]
    pad = jnp.zeros((D_MODEL, LAT_WIDTH - LAT_KPE - 2 * ROPE), BF16)
    w_lat = jnp.concatenate([lat, _swap_halves(k_pe), pad], axis=1)
    return w[:, lat_end + POOL_WIDTH:], w[:, lat_end:lat_end + POOL_WIDTH], w_lat


def _prep_w_q(w_q_b):
    w = w_q_b.reshape(Q_LORA, N_HEADS, NOPE + ROPE).transpose(1, 0, 2)
    pe = w[..., NOPE:]
    return jnp.concatenate([w, _swap_halves(pe)], axis=-1).astype(BF16)


def _prep_w_kv(w_kv_b):
    return w_kv_b.reshape(KV_LORA, N_HEADS, NOPE + V_DIM).transpose(1, 0, 2).astype(BF16)


def kernel(x, mem, positions, g_mix_pre, w_in, g_q_a, w_q_b, g_kv_a, w_kv_b, w_attn_br,
           w_pool_grp, pool_scale, w_pool_br, w_mix_out, g_mix_post, g_mem_pre, g_mem_kv,
           w_mem_q, w_mem_kv, w_mem_o, g_mem_post, g_ffn_pre, w_ffn_in, conv_w, conv_b,
           w_ffn_out, g_ffn_post):
    batch, seq, d = x.shape
    depth = w_in.shape[0]
    assert (batch, seq, d) == (1, SEQ, D_MODEL)
    inv_freq = ROPE_THETA ** (-np.arange(0, ROPE, 2, dtype=np.float32) / ROPE)
    freq = jnp.asarray(np.tile(inv_freq.astype(np.float32), 4)[None, :])
    pos = positions.reshape(seq, 1).astype(F32)
    xs = x.reshape(seq, d)
    mems = mem.reshape(N_MEM, d)
    for l in range(depth):
        z_gates, z_pool, z_lat = _in_proj(xs, g_mix_pre[l][None], *_prep_w_in(w_in[l]),
                                          **CALLS["in_proj"])
        q, kn, kpe, v, wab, wmo, wpb = _mla_proj(
            z_lat, pos, freq, g_q_a[l][None], g_kv_a[l][None], _prep_w_q(w_q_b[l]),
            _prep_w_kv(w_kv_b[l]), (w_attn_br[l], w_mix_out[l], w_pool_br[l]),
            **CALLS["mla_proj"])
        o = _flash(q, kn, kpe, v, **CALLS["flash"])
        x1, wfi, wmq, wmo_mem, wmkv = _mix(
            o, z_gates, z_pool, xs, wab, w_pool_grp[l].astype(BF16), pool_scale[l][None],
            wpb, wmo, g_mix_post[l][None],
            (w_ffn_in[l], w_mem_q[l], w_mem_o[l], w_mem_kv[l]), **CALLS["mix"])
        kv = _mem_kv(mems, g_mem_kv[l][None], wmkv, **CALLS["mem_kv"])
        x2, wfo = _mem_attn(x1, g_mem_pre[l][None], wmq, kv, wmo_mem, g_mem_post[l][None],
                            (w_ffn_out[l],), **CALLS["mem_attn"])
        xs = _ffn(x2, g_ffn_pre[l][None], wfi, conv_w[l], conv_b[l][None], wfo,
                  g_ffn_post[l][None], **CALLS["ffn"])
    return xs.reshape(batch, seq, d)
```

```python
import functools
import math

import numpy as np
import jax
import jax.numpy as jnp
from jax import lax
from jax.experimental import pallas as pl
from jax.experimental.pallas import tpu as pltpu

F32 = jnp.float32
BF16 = jnp.bfloat16

D_MODEL = 2048
SEQ = 16384
N_HEADS = 16
Q_LORA = 512
KV_LORA = 512
NOPE = 128
ROPE = 64
V_DIM = 128
ROPE_THETA = 10000.0
ATTN_WIDTH = N_HEADS * V_DIM
POOL_WIDTH = 1024
POOL_WINDOWS = (2, 4, 8, 16)
POOL_GROUP_DIM = 256
N_MEM = 256
MEM_HEADS = 4
MEM_HEAD_DIM = 512
D_FF = 5632
EPS = 1e-6

LAT_Q = 0
LAT_KV = 512
LAT_KPE = 1024
LAT_WIDTH = 1280

HEAD_PAD = 256
HALO = 16
MIB = 1024 * 1024

CALLS = {
    "prep_w_in": dict(rows=512, vmem_mib=48),
    "prep_w_heads": dict(vmem_mib=48),
    "in_proj": dict(tm=1024, tn=1024, vmem_mib=58),
    "mla_proj": dict(tm=512, vmem_mib=48),
    "flash": dict(tq=512, tk=1024, rb=8, group=4, lag=1, vmem_mib=58),
    "mix": dict(tm=256, vmem_mib=56),
    "mem_kv": dict(tn=1024, vmem_mib=32),
    "mem_attn": dict(tm=512, vmem_mib=48),
    "ffn": dict(tm=512, tn=512, tiles_per_step=2, vmem_mib=58),
}


def _cparams(sem, vmem_mib):
    return pltpu.CompilerParams(dimension_semantics=sem,
                                vmem_limit_bytes=int(vmem_mib * MIB))


def _rms(xf, g):
    ms = jnp.mean(xf * xf, axis=-1, keepdims=True)
    return xf * lax.rsqrt(ms + EPS) * g


def _norm_rows(dst_ref, dst_off, src_ref, g, rows, chunk):
    def body(c, carry):
        r = pl.multiple_of(c * chunk, chunk)
        y = _rms(src_ref[pl.ds(r, chunk), :].astype(F32), g)
        dst_ref[pl.ds(dst_off + r, chunk), :] = y.astype(dst_ref.dtype)
        return carry
    lax.fori_loop(0, rows // chunk, body, 0)


def _with_casts(kernel, n_in, n_out, n_cast):
    def wrapped(*refs):
        ins, rest = refs[:n_in], refs[n_in:]
        cast_in, rest = rest[:n_cast], rest[n_cast:]
        outs, rest = rest[:n_out], rest[n_out:]
        cast_out, scratch = rest[:n_cast], rest[n_cast:]
        for src, dst in zip(cast_in, cast_out):
            dst[...] = src[...].astype(BF16)
        kernel(*ins, *outs, *scratch)
    return wrapped


def _cast_specs(weights, steps):
    specs, shapes = [], []
    for w in weights:
        rows = w.shape[0] // steps
        assert rows * steps == w.shape[0] and rows % 16 == 0
        specs.append(pl.BlockSpec((rows, w.shape[1]), lambda i: (i, 0)))
        shapes.append(jax.ShapeDtypeStruct(w.shape, BF16))
    return specs, shapes


def _in_proj_kernel(x_ref, g_ref, wg_ref, wp_ref, wl_ref, zg_ref, zp_ref, zl_ref, h_ref,
                    *, tm, gate_tiles):
    j = pl.program_id(1)

    @pl.when(j == 0)
    def _():
        _norm_rows(h_ref, 0, x_ref, g_ref[...], tm, 128)

    def project(w_ref, z_ref):
        z_ref[...] = lax.dot_general(h_ref[...], w_ref[...], (((1,), (1,)), ((), ())),
                                     preferred_element_type=F32).astype(z_ref.dtype)

    pl.when(j < gate_tiles)(lambda: project(wg_ref, zg_ref))
    pl.when(j == gate_tiles)(lambda: project(wp_ref, zp_ref))
    pl.when(j == gate_tiles + 1)(lambda: project(wl_ref, zl_ref))


def _in_proj(x, g, w_gates, w_pool, w_lat, *, tm, tn, vmem_mib):
    s, d = x.shape
    n_gates, n_pool, n_lat = w_gates.shape[0], w_pool.shape[0], w_lat.shape[0]
    gate_tiles = n_gates // tn

    def gate_tile(j):
        return jnp.minimum(j, gate_tiles - 1)
    return pl.pallas_call(
        functools.partial(_in_proj_kernel, tm=tm, gate_tiles=gate_tiles),
        grid=(s // tm, gate_tiles + 2),
        in_specs=[pl.BlockSpec((tm, d), lambda i, j: (i, 0)),
                  pl.BlockSpec((1, d), lambda i, j: (0, 0)),
                  pl.BlockSpec((tn, d), lambda i, j: (gate_tile(j), 0)),
                  pl.BlockSpec((n_pool, d), lambda i, j: (0, 0), pipeline_mode=pl.Buffered(1)),
                  pl.BlockSpec((n_lat, d), lambda i, j: (0, 0), pipeline_mode=pl.Buffered(1))],
        out_specs=[pl.BlockSpec((tm, tn), lambda i, j: (i, gate_tile(j))),
                   pl.BlockSpec((tm, n_pool), lambda i, j: (i, 0)),
                   pl.BlockSpec((tm, n_lat), lambda i, j: (i, 0))],
        out_shape=[jax.ShapeDtypeStruct((s, n_gates), BF16),
                   jax.ShapeDtypeStruct((s, n_pool), BF16),
                   jax.ShapeDtypeStruct((s, n_lat), BF16)],
        scratch_shapes=[pltpu.VMEM((tm, d), BF16)],
        compiler_params=_cparams(("parallel", "arbitrary"), vmem_mib),
        name="in_proj",
    )(x, g, w_gates, w_pool, w_lat)


def _mla_proj_kernel(ql_ref, ckv_ref, kp_ref, pos_ref, freq_ref, gq_ref, gkv_ref,
                     wq_ref, wkv_ref, q_ref, kn_ref, kpe_ref, v_ref, *, q_scale):
    qn = _rms(ql_ref[...].astype(F32), gq_ref[...]).astype(BF16)
    cn = _rms(ckv_ref[...].astype(F32), gkv_ref[...]).astype(BF16)
    ang = pos_ref[...] * freq_ref[...]
    lane = lax.broadcasted_iota(jnp.int32, (1, 2 * ROPE), 1)
    sign = jnp.where((lane >= ROPE) & (lane < ROPE + ROPE // 2), -1.0, 1.0)
    table = jnp.where(lane < ROPE, jnp.cos(ang), jnp.sin(ang) * sign)
    low = lane < ROPE

    def rot(t):
        t = t * table
        return t + pltpu.roll(t, ROPE, axis=1)

    kpe_ref[...] = jnp.where(low, rot(kp_ref[...].astype(F32)), 0.0).astype(BF16)
    for h in range(N_HEADS):
        qh = jnp.dot(qn, wq_ref[h], preferred_element_type=F32)
        q_ref[h, :, 0:NOPE] = (qh[:, 0:NOPE] * q_scale).astype(BF16)
        q_ref[h, :, NOPE:HEAD_PAD] = (rot(qh[:, NOPE:HEAD_PAD]) * q_scale).astype(BF16)
        kvh = jnp.dot(cn, wkv_ref[h], preferred_element_type=F32)
        kn_ref[h] = kvh[:, 0:NOPE].astype(BF16)
        v_ref[h] = kvh[:, NOPE:HEAD_PAD].astype(BF16)


def _mla_proj(z_lat, pos, freq, gq, gkv, wq, wkv, cast, *, tm, vmem_mib):
    s = z_lat.shape[0]
    q_scale = (NOPE + ROPE) ** -0.5 * math.log2(math.e)
    cast_specs, cast_shapes = _cast_specs(cast, s // tm)

    def head_out(width):
        return jax.ShapeDtypeStruct((N_HEADS, s, width), BF16)

    def head_spec(width):
        return pl.BlockSpec((N_HEADS, tm, width), lambda i: (0, i, 0))
    const3 = lambda i: (0, 0, 0)
    return pl.pallas_call(
        _with_casts(functools.partial(_mla_proj_kernel, q_scale=q_scale), 9, 4, len(cast)),
        grid=(s // tm,),
        in_specs=[pl.BlockSpec((tm, Q_LORA), lambda i: (i, LAT_Q // Q_LORA)),
                  pl.BlockSpec((tm, KV_LORA), lambda i: (i, LAT_KV // KV_LORA)),
                  pl.BlockSpec((tm, 2 * ROPE), lambda i: (i, LAT_KPE // (2 * ROPE))),
                  pl.BlockSpec((tm, 1), lambda i: (i, 0)),
                  pl.BlockSpec((1, 2 * ROPE), lambda i: (0, 0)),
                  pl.BlockSpec((1, Q_LORA), lambda i: (0, 0)),
                  pl.BlockSpec((1, KV_LORA), lambda i: (0, 0)),
                  pl.BlockSpec((N_HEADS, Q_LORA, HEAD_PAD), const3,
                               pipeline_mode=pl.Buffered(1)),
                  pl.BlockSpec((N_HEADS, KV_LORA, HEAD_PAD), const3,
                               pipeline_mode=pl.Buffered(1))] + cast_specs,
        out_specs=[head_spec(HEAD_PAD), head_spec(NOPE),
                   pl.BlockSpec((tm, HEAD_PAD - NOPE), lambda i: (i, 0)),
                   head_spec(V_DIM)] + cast_specs,
        out_shape=[head_out(HEAD_PAD), head_out(NOPE),
                   jax.ShapeDtypeStruct((s, HEAD_PAD - NOPE), BF16),
                   head_out(V_DIM)] + cast_shapes,
        compiler_params=_cparams(("parallel",), vmem_mib),
        name="mla_proj",
    )(z_lat, z_lat, z_lat, pos, freq, gq, gkv, wq, wkv, *cast)


M_INIT = float(np.finfo(np.float32).min)
EDGE_BLOCK = 3


def _flash_kernel(q_ref, kn_ref, kpe_ref, v_ref, o_ref, *scratch, tq, tk, rb, group, lag):
    n_s = 2 * lag
    s_bufs, (p0, p1, a0, a1, m_ref, acc_ref) = scratch[:n_s], scratch[n_s:]
    p_bufs, a_bufs = (p0, p1), (a0, a1)
    r = tk // tq
    nb = rb // r
    band = pl.program_id(1)
    row0 = band * rb
    row_end = row0 + rb
    n_plain = r * nb * nb * band + r * (nb * (nb - 1) // 2)
    p1[...] = jnp.zeros(p1.shape, BF16)
    a1[...] = jnp.ones(a1.shape, F32)
    acc_ref[...] = jnp.zeros(acc_ref.shape, F32)
    m_ref[...] = jnp.full(m_ref.shape, M_INIT, F32)

    def last_tile(i):
        return lax.div(i, jnp.int32(r))

    def stage_a(e, row, j, width=tk):
        q = q_ref[0, pl.ds(pl.multiple_of(row * tq, tq), tq), :]
        cols = pl.ds(pl.multiple_of(j * tk, tk), width)
        kt = jnp.concatenate([kn_ref[0, cols, :], kpe_ref[cols, :]], axis=1)
        s_bufs[e % n_s][:, 0:width] = lax.dot_general(q, kt, (((1,), (1,)), ((), ())),
                                                      preferred_element_type=F32)

    def stage_b(e, row, width=tk, off=None):
        s = s_bufs[e % n_s][:, 0:width]
        if off is not None:
            rel = (lax.broadcasted_iota(jnp.int32, (tq, width), 1)
                   - lax.broadcasted_iota(jnp.int32, (tq, width), 0))
            s = jnp.where(rel <= off, s, -jnp.inf)
        m_prev = m_ref[row]
        m_new = jnp.maximum(m_prev, jnp.max(s, axis=-1, keepdims=True))
        a_bufs[e % 2][...] = jnp.exp2(m_prev - m_new)
        p_bufs[e % 2][:, 0:width] = jnp.exp2(s - m_new).astype(BF16)
        m_ref[row] = m_new

    def stage_c(e, row, j, width=tk):
        vt = jnp.concatenate([v_ref[0, pl.ds(pl.multiple_of(j * tk, tk), width), :],
                              jnp.ones((width, HEAD_PAD - V_DIM), BF16)], axis=1)
        acc = a_bufs[e % 2][...] * acc_ref[row] + jnp.dot(
            p_bufs[e % 2][:, 0:width], vt, preferred_element_type=F32)
        acc_ref[row] = acc
        return acc

    def advance(i, j):
        lt = last_tile(i)
        same_row = j + 1 < lt
        more_rows = i + 1 < row_end
        i_plain = jnp.where(same_row, i, jnp.where(more_rows, i + 1, row0))
        j_plain = jnp.where(same_row, j + 1, jnp.where(more_rows, 0, last_tile(row0)))
        on_edge = j == lt
        return (jnp.where(on_edge, i + 1, i_plain),
                jnp.where(on_edge, last_tile(i + 1), j_plain))

    first = jnp.maximum(row0, r)
    pairs = [(first, jnp.int32(0))]
    for _ in range(lag + 1):
        pairs.append(advance(*pairs[-1]))
    for e in range(lag):
        stage_a(e, pairs[e][0] - row0, pairs[e][1])

    def plain_group(_, state):
        state = list(state)
        for g in range(group):
            (ia, ja), (ib, jb), (ic, jc) = state[0], state[lag], state[lag + 1]
            stage_a(lag + g, ia - row0, ja)
            stage_b(g, ib - row0)
            stage_c(g - 1, ic - row0, jc)
            state = [advance(ia, ja)] + state[:-1]
        return tuple(state)

    state = tuple(pairs[lag - d] for d in range(lag + 1)) + (pairs[0],)
    state = lax.fori_loop(0, n_plain // group, plain_group, state)

    ic0, jc0 = state[lag + 1]
    jb0 = last_tile(row0)

    def edge_tick(k):
        if lag <= k + lag < rb:
            ka = k + lag
            stage_a(ka, ka, jb0 + ka // r, (ka % r + 1) * tq)
        if k < rb:
            stage_b(k, k, (k % r + 1) * tq, (k % r) * tq)
        if k == 0:
            stage_c(-1, ic0 - row0, jc0)
        else:
            kc = k - 1
            acc = stage_c(kc, kc, jb0 + kc // r, (kc % r + 1) * tq)
            o_ref[kc * tq:k * tq, :] = (
                acc[:, 0:V_DIM] / acc[:, V_DIM:HEAD_PAD]).astype(o_ref.dtype)

    for k0 in range(0, rb + 1, EDGE_BLOCK):
        @pl.when(band >= 0)
        def _():
            for k in range(k0, min(k0 + EDGE_BLOCK, rb + 1)):
                edge_tick(k)


def _flash(q, kn, kpe, v, *, tq, tk, rb, group, lag, vmem_mib):
    h, s, _ = q.shape
    band = tq * rb
    r = tk // tq
    nb = rb // r
    assert tk % tq == 0 and rb % r == 0 and s % band == 0
    assert group % (2 * lag) == 0 and nb >= 2 and lag <= rb
    assert (r * nb * nb) % group == 0 and (r * (nb * (nb - 1) // 2)) % group == 0
    return pl.pallas_call(
        functools.partial(_flash_kernel, tq=tq, tk=tk, rb=rb, group=group, lag=lag),
        grid=(h, s // band),
        in_specs=[pl.BlockSpec((1, band, HEAD_PAD), lambda hh, b: (hh, b, 0)),
                  pl.BlockSpec((1, s, NOPE), lambda hh, b: (hh, 0, 0)),
                  pl.BlockSpec((s, HEAD_PAD - NOPE), lambda hh, b: (0, 0),
                               pipeline_mode=pl.Buffered(1)),
                  pl.BlockSpec((1, s, V_DIM), lambda hh, b: (hh, 0, 0))],
        out_specs=pl.BlockSpec((band, V_DIM), lambda hh, b: (b, hh)),
        out_shape=jax.ShapeDtypeStruct((s, h * V_DIM), BF16),
        scratch_shapes=[pltpu.VMEM((tq, tk), F32)] * (2 * lag) + [
                        pltpu.VMEM((tq, tk), BF16), pltpu.VMEM((tq, tk), BF16),
                        pltpu.VMEM((tq, 1), F32), pltpu.VMEM((tq, 1), F32),
                        pltpu.VMEM((rb, tq, 1), F32),
                        pltpu.VMEM((rb, tq, HEAD_PAD), F32)],
        compiler_params=_cparams(("parallel", "arbitrary"), vmem_mib),
        name="flash",
    )(q, kn, kpe, v)


def _mix_kernel(o_ref, ga_ref, gp_ref, u_ref, uh_ref, x_ref, wab_ref, wpg_ref, ps_ref,
                wpb_ref, wmo_ref, g_ref, out_ref, ext_ref, yp_ref, *, tm):
    i = pl.program_id(0)
    ya = jnp.dot(o_ref[...], wab_ref[...], preferred_element_type=F32)
    ext_ref[0:HALO, :] = jnp.where(i == 0, 0.0, uh_ref[...].astype(F32))
    ext_ref[HALO:HALO + tm, :] = u_ref[...].astype(F32)
    t = i * tm + lax.broadcasted_iota(jnp.int32, (tm, 1), 0)
    for g, w in enumerate(POOL_WINDOWS):
        cols = slice(g * POOL_GROUP_DIM, (g + 1) * POOL_GROUP_DIM)
        cur = ext_ref[HALO:HALO + tm, cols]
        win = cur
        for b in range(1, w):
            win = win + ext_ref[HALO - b:HALO - b + tm, cols]
        cnt = jnp.minimum(t + 1, w).astype(F32)
        d = win / cnt - cur
        yg = jnp.dot(d.astype(BF16), wpg_ref[g], preferred_element_type=F32)
        yp_ref[:, cols] = (yg * ps_ref[:, cols]).astype(BF16)
    yp = jnp.dot(yp_ref[...], wpb_ref[...], preferred_element_type=F32)
    merged = (jax.nn.sigmoid(ga_ref[...].astype(F32)) * ya
              + jax.nn.sigmoid(gp_ref[...].astype(F32)) * yp)
    mo = jnp.dot(merged.astype(BF16), wmo_ref[...], preferred_element_type=F32)
    out_ref[...] = x_ref[...] + _rms(mo, g_ref[...])


def _resident(shape):
    zeros = (0,) * len(shape)
    return pl.BlockSpec(shape, lambda i: zeros, pipeline_mode=pl.Buffered(1))


def _mix(o, z_gates, z_pool, x, wab, wpg, ps, wpb, wmo, g, cast, *, tm, vmem_mib):
    s, d = x.shape
    cast_specs, cast_shapes = _cast_specs(cast, s // tm)
    return pl.pallas_call(
        _with_casts(functools.partial(_mix_kernel, tm=tm), 12, 1, len(cast)),
        grid=(s // tm,),
        in_specs=[pl.BlockSpec((tm, ATTN_WIDTH), lambda i: (i, 0)),
                  pl.BlockSpec((tm, d), lambda i: (i, 0)),
                  pl.BlockSpec((tm, d), lambda i: (i, 1)),
                  pl.BlockSpec((tm, POOL_WIDTH), lambda i: (i, 0)),
                  pl.BlockSpec((HALO, POOL_WIDTH),
                               lambda i: (jnp.maximum(i * (tm // HALO) - 1, 0), 0)),
                  pl.BlockSpec((tm, d), lambda i: (i, 0)),
                  _resident(wab.shape), _resident(wpg.shape), _resident(ps.shape),
                  _resident(wpb.shape), _resident(wmo.shape), _resident(g.shape)]
        + cast_specs,
        out_specs=[pl.BlockSpec((tm, d), lambda i: (i, 0))] + cast_specs,
        out_shape=[jax.ShapeDtypeStruct((s, d), F32)] + cast_shapes,
        scratch_shapes=[pltpu.VMEM((HALO + tm, POOL_WIDTH), F32),
                        pltpu.VMEM((tm, POOL_WIDTH), BF16)],
        compiler_params=_cparams(("parallel",), vmem_mib),
        name="mix",
    )(o, z_gates, z_gates, z_pool, z_pool, x, wab, wpg, ps, wpb, wmo, g, *cast)


def _mem_kv_kernel(mem_ref, g_ref, w_ref, kv_ref):
    mn = _rms(mem_ref[...], g_ref[...]).astype(BF16)
    kv_ref[...] = jnp.dot(mn, w_ref[...], preferred_element_type=F32).astype(kv_ref.dtype)


def _mem_kv(mem, g, w, *, tn, vmem_mib):
    m, d = mem.shape
    n = w.shape[1]
    return pl.pallas_call(
        _mem_kv_kernel,
        grid=(n // tn,),
        in_specs=[pl.BlockSpec((m, d), lambda j: (0, 0)),
                  pl.BlockSpec((1, d), lambda j: (0, 0)),
                  pl.BlockSpec((d, tn), lambda j: (0, j))],
        out_specs=pl.BlockSpec((m, tn), lambda j: (0, j)),
        out_shape=jax.ShapeDtypeStruct((m, n), BF16),
        compiler_params=_cparams(("parallel",), vmem_mib),
        name="mem_kv",
    )(mem, g, w)


def _mem_attn_kernel(x_ref, gpre_ref, wq_ref, kv_ref, wo_ref, gpost_ref, out_ref,
                     o_ref):
    x = x_ref[...]
    h = _rms(x, gpre_ref[...]).astype(BF16)
    scale = MEM_HEAD_DIM ** -0.5
    q = (jnp.dot(h, wq_ref[...], preferred_element_type=F32) * scale).astype(BF16)
    for hd in range(MEM_HEADS):
        cols = slice(hd * MEM_HEAD_DIM, (hd + 1) * MEM_HEAD_DIM)
        vcols = slice(D_MODEL + hd * MEM_HEAD_DIM, D_MODEL + (hd + 1) * MEM_HEAD_DIM)
        s = lax.dot_general(q[:, cols], kv_ref[:, cols], (((1,), (1,)), ((), ())),
                            preferred_element_type=F32)
        p = jnp.exp(s - jnp.max(s, axis=-1, keepdims=True))
        l = jnp.sum(p, axis=-1, keepdims=True)
        oh = jnp.dot(p.astype(BF16), kv_ref[:, vcols], preferred_element_type=F32)
        o_ref[:, cols] = (oh / l).astype(BF16)
    y = jnp.dot(o_ref[...], wo_ref[...], preferred_element_type=F32)
    out_ref[...] = x + _rms(y, gpost_ref[...])


def _mem_attn(x, gpre, wq, kv, wo, gpost, cast, *, tm, vmem_mib):
    s, d = x.shape
    cast_specs, cast_shapes = _cast_specs(cast, s // tm)
    return pl.pallas_call(
        _with_casts(_mem_attn_kernel, 6, 1, len(cast)),
        grid=(s // tm,),
        in_specs=[pl.BlockSpec((tm, d), lambda i: (i, 0)),
                  _resident(gpre.shape), _resident(wq.shape), _resident(kv.shape),
                  _resident(wo.shape), _resident(gpost.shape)] + cast_specs,
        out_specs=[pl.BlockSpec((tm, d), lambda i: (i, 0))] + cast_specs,
        out_shape=[jax.ShapeDtypeStruct((s, d), F32)] + cast_shapes,
        scratch_shapes=[pltpu.VMEM((tm, d), BF16)],
        compiler_params=_cparams(("parallel",), vmem_mib),
        name="mem_attn",
    )(x, gpre, wq, kv, wo, gpost, *cast)


def _gelu_tanh(x):
    c = math.sqrt(2.0 / math.pi)
    return 0.5 * x * (1.0 + jnp.tanh(c * (x + 0.044715 * (x * x * x))))


FFN_TILE_REFS = 7


def _ffn_kernel(x_ref, xh_ref, gpre_ref, *refs, tm, tiles_per_step, n_tiles):
    n_in = FFN_TILE_REFS * tiles_per_step
    tile_refs = [refs[FFN_TILE_REFS * t:FFN_TILE_REFS * (t + 1)] for t in range(tiles_per_step)]
    gpost_ref, out_ref, h_ref = refs[n_in:n_in + 3]
    a_refs = refs[n_in + 3:]
    i = pl.program_id(0)
    j = pl.program_id(1)

    @pl.when(j == 0)
    def _():
        hh = _rms(xh_ref[...], gpre_ref[...])
        h_ref[0:HALO, :] = jnp.where(i == 0, 0.0, hh).astype(BF16)
        _norm_rows(h_ref, HALO, x_ref, gpre_ref[...], tm, 128)
        out_ref[...] = jnp.zeros(out_ref.shape, F32)

    def conv(a_ref, w_ref, cw_ref, cb_ref):
        a_ref[...] = jnp.dot(h_ref[...], w_ref[...], preferred_element_type=F32)
        return (cb_ref[...]
                + a_ref[HALO - 2:HALO - 2 + tm, :] * cw_ref[0:1, :]
                + a_ref[HALO - 1:HALO - 1 + tm, :] * cw_ref[1:2, :]
                + a_ref[HALO:HALO + tm, :] * cw_ref[2:3, :])

    def tile_update(t):
        wg_ref, wu_ref, cwg_ref, cwu_ref, cbg_ref, cbu_ref, wo_ref = tile_refs[t]
        act = _gelu_tanh(conv(a_refs[t], wg_ref, cwg_ref, cbg_ref))
        up = conv(a_refs[t], wu_ref, cwu_ref, cbu_ref)
        return jnp.dot((act * up).astype(BF16), wo_ref[...], preferred_element_type=F32)

    def step(n):
        upd = tile_update(0)
        for t in range(1, n):
            upd = upd + tile_update(t)
        out_ref[...] += upd

    full_steps, rest = divmod(n_tiles, tiles_per_step)
    pl.when(j < full_steps)(lambda: step(tiles_per_step))
    if rest:
        pl.when(j == full_steps)(lambda: step(rest))

    @pl.when(j == pl.num_programs(1) - 1)
    def _():
        out_ref[...] = x_ref[...] + _rms(out_ref[...], gpost_ref[...])


def _ffn(x, gpre, w_in, conv_w, conv_b, w_out, gpost, *, tm, tn, tiles_per_step, vmem_mib):
    s, d = x.shape
    n_tiles = D_FF // tn

    def tile_specs(t):
        def tile(j):
            return jnp.minimum(j * tiles_per_step + t, n_tiles - 1)
        return [pl.BlockSpec((d, tn), lambda i, j: (0, tile(j))),
                pl.BlockSpec((d, tn), lambda i, j: (0, n_tiles + tile(j))),
                pl.BlockSpec((3, tn), lambda i, j: (0, tile(j))),
                pl.BlockSpec((3, tn), lambda i, j: (0, n_tiles + tile(j))),
                pl.BlockSpec((1, tn), lambda i, j: (0, tile(j))),
                pl.BlockSpec((1, tn), lambda i, j: (0, n_tiles + tile(j))),
                pl.BlockSpec((tn, d), lambda i, j: (tile(j), 0))]
    tile_args = (w_in, w_in, conv_w, conv_w, conv_b, conv_b, w_out)
    return pl.pallas_call(
        functools.partial(_ffn_kernel, tm=tm, tiles_per_step=tiles_per_step,
                          n_tiles=n_tiles),
        grid=(s // tm, pl.cdiv(n_tiles, tiles_per_step)),
        in_specs=[pl.BlockSpec((tm, d), lambda i, j: (i, 0)),
                  pl.BlockSpec((HALO, d),
                               lambda i, j: (jnp.maximum(i * (tm // HALO) - 1, 0), 0)),
                  pl.BlockSpec((1, d), lambda i, j: (0, 0))]
        + [spec for t in range(tiles_per_step) for spec in tile_specs(t)]
        + [pl.BlockSpec((1, d), lambda i, j: (0, 0))],
        out_specs=pl.BlockSpec((tm, d), lambda i, j: (i, 0)),
        out_shape=jax.ShapeDtypeStruct((s, d), F32),
        scratch_shapes=[pltpu.VMEM((HALO + tm, d), BF16)]
        + [pltpu.VMEM((HALO + tm, tn), F32)] * tiles_per_step,
        compiler_params=_cparams(("parallel", "arbitrary"), vmem_mib),
        name="ffn",
    )(x, x, gpre, *(tile_args * tiles_per_step), gpost)


W_IN_LAT_ROWS = Q_LORA + KV_LORA + ROPE
W_IN_GATE_ROW0 = W_IN_LAT_ROWS + POOL_WIDTH


def _prep_w_in_kernel(gates_ref, pool_ref, lat_ref, wg_ref, wp_ref, wl_ref, *, pool_steps):
    i = pl.program_id(0)
    wg_ref[...] = gates_ref[...].astype(BF16)

    @pl.when(i < pool_steps)
    def _():
        wp_ref[...] = pool_ref[...].astype(BF16)

    @pl.when(i == 0)
    def _():
        half = ROPE // 2
        k_pe = W_IN_LAT_ROWS - ROPE
        wl_ref[0:W_IN_LAT_ROWS, :] = lat_ref[...].astype(BF16)
        wl_ref[W_IN_LAT_ROWS:W_IN_LAT_ROWS + half, :] = (
            lat_ref[k_pe + half:W_IN_LAT_ROWS, :].astype(BF16))
        wl_ref[W_IN_LAT_ROWS + half:W_IN_LAT_ROWS + ROPE, :] = (
            lat_ref[k_pe:k_pe + half, :].astype(BF16))
        wl_ref[W_IN_LAT_ROWS + ROPE:LAT_WIDTH, :] = jnp.zeros(
            (LAT_WIDTH - W_IN_LAT_ROWS - ROPE, wl_ref.shape[1]), BF16)


def _prep_w_in(w_in_t, *, rows, vmem_mib):
    n, d = w_in_t.shape
    n_gates = n - W_IN_GATE_ROW0
    steps, pool_steps = n_gates // rows, POOL_WIDTH // rows

    def pool_block(i):
        return jnp.minimum(i, pool_steps - 1)
    return pl.pallas_call(
        functools.partial(_prep_w_in_kernel, pool_steps=pool_steps),
        grid=(steps,),
        in_specs=[pl.BlockSpec((pl.Element(rows), pl.Element(d)),
                               lambda i: (pl.multiple_of(W_IN_GATE_ROW0 + rows * i, HALO), 0)),
                  pl.BlockSpec((pl.Element(rows), pl.Element(d)),
                               lambda i: (pl.multiple_of(
                                   W_IN_LAT_ROWS + rows * pool_block(i), HALO), 0)),
                  pl.BlockSpec((pl.Element(W_IN_LAT_ROWS), pl.Element(d)), lambda i: (0, 0),
                               pipeline_mode=pl.Buffered(1))],
        out_specs=[pl.BlockSpec((rows, d), lambda i: (i, 0)),
                   pl.BlockSpec((rows, d), lambda i: (pool_block(i), 0)),
                   pl.BlockSpec((LAT_WIDTH, d), lambda i: (0, 0))],
        out_shape=[jax.ShapeDtypeStruct((n_gates, d), BF16),
                   jax.ShapeDtypeStruct((POOL_WIDTH, d), BF16),
                   jax.ShapeDtypeStruct((LAT_WIDTH, d), BF16)],
        compiler_params=_cparams(("arbitrary",), vmem_mib),
        name="prep_w_in",
    )(w_in_t, w_in_t, w_in_t)


def _prep_w_heads_kernel(wq_ref, wkv_ref, wq_out, wkv_out):
    half = ROPE // 2
    wq = wq_ref[...]
    for h in range(N_HEADS):
        c = h * (NOPE + ROPE)
        pe = c + NOPE
        wq_out[h] = jnp.concatenate(
            [wq[:, c:pe + ROPE], wq[:, pe + half:pe + ROPE], wq[:, pe:pe + half]],
            axis=1).astype(BF16)
        wkv_out[h] = wkv_ref[:, h * HEAD_PAD:(h + 1) * HEAD_PAD].astype(BF16)


def _prep_w_heads(w_q_b, w_kv_b, *, vmem_mib):
    out = jax.ShapeDtypeStruct((N_HEADS, Q_LORA, HEAD_PAD), BF16)
    return pl.pallas_call(
        _prep_w_heads_kernel,
        out_shape=[out, out],
        compiler_params=pltpu.CompilerParams(vmem_limit_bytes=int(vmem_mib * MIB)),
        name="prep_w_heads",
    )(w_q_b, w_kv_b)


def kernel(x, mem, positions, g_mix_pre, w_in, g_q_a, w_q_b, g_kv_a, w_kv_b, w_attn_br,
           w_pool_grp, pool_scale, w_pool_br, w_mix_out, g_mix_post, g_mem_pre, g_mem_kv,
           w_mem_q, w_mem_kv, w_mem_o, g_mem_post, g_ffn_pre, w_ffn_in, conv_w, conv_b,
           w_ffn_out, g_ffn_post):
    batch, seq, d = x.shape
    depth = w_in.shape[0]
    assert (batch, seq, d) == (1, SEQ, D_MODEL)
    inv_freq = ROPE_THETA ** (-np.arange(0, ROPE, 2, dtype=np.float32) / ROPE)
    freq = jnp.asarray(np.tile(inv_freq.astype(np.float32), 4)[None, :])
    pos = positions.reshape(seq, 1).astype(F32)
    xs = x.reshape(seq, d)
    mems = mem.reshape(N_MEM, d)
    for l in range(depth):
        z_gates, z_pool, z_lat = _in_proj(xs, g_mix_pre[l][None],
                                          *_prep_w_in(w_in[l].T, **CALLS["prep_w_in"]),
                                          **CALLS["in_proj"])
        wq, wkv = _prep_w_heads(w_q_b[l], w_kv_b[l], **CALLS["prep_w_heads"])
        q, kn, kpe, v, wab, wmo, wpb = _mla_proj(
            z_lat, pos, freq, g_q_a[l][None], g_kv_a[l][None], wq, wkv,
            (w_attn_br[l], w_mix_out[l], w_pool_br[l]), **CALLS["mla_proj"])
        o = _flash(q, kn, kpe, v, **CALLS["flash"])
        x1, wfi, wmq, wmo_mem, wmkv = _mix(
            o, z_gates, z_pool, xs, wab, w_pool_grp[l].astype(BF16), pool_scale[l][None],
            wpb, wmo, g_mix_post[l][None],
            (w_ffn_in[l], w_mem_q[l], w_mem_o[l], w_mem_kv[l]), **CALLS["mix"])
        kv = _mem_kv(mems, g_mem_kv[l][None], wmkv, **CALLS["mem_kv"])
        x2, wfo = _mem_attn(x1, g_mem_pre[l][None], wmq, kv, wmo_mem, g_mem_post[l][None],
                            (w_ffn_out[l],), **CALLS["mem_attn"])
        xs = _ffn(x2, g_ffn_pre[l][None], wfi, conv_w[l], conv_b[l][None], wfo,
                  g_ffn_post[l][None], **CALLS["ffn"])
    return xs.reshape(batch, seq, d)
```

```python
import functools
import math

import numpy as np
import jax
import jax.numpy as jnp
from jax import lax
from jax.experimental import pallas as pl
from jax.experimental.pallas import tpu as pltpu

F32 = jnp.float32
BF16 = jnp.bfloat16

D_MODEL = 2048
SEQ = 16384
N_HEADS = 16
Q_LORA = 512
KV_LORA = 512
NOPE = 128
ROPE = 64
V_DIM = 128
ROPE_THETA = 10000.0
ATTN_WIDTH = N_HEADS * V_DIM
POOL_WIDTH = 1024
POOL_WINDOWS = (2, 4, 8, 16)
POOL_GROUP_DIM = 256
N_MEM = 256
MEM_HEADS = 4
MEM_HEAD_DIM = 512
D_FF = 5632
EPS = 1e-6

LAT_Q = 0
LAT_KV = 512
LAT_KPE = 1024
LAT_WIDTH = 1280

HEAD_PAD = 256
BF16_ROWS = 16
HALO = BF16_ROWS
MIB = 1024 * 1024

CALLS = {
    "prep_w_in": dict(rows=512, vmem_mib=48),
    "prep_w_heads": dict(vmem_mib=48),
    "in_proj": dict(tm=1024, tn=1024, vmem_mib=58),
    "mla_proj": dict(tm=512, vmem_mib=48),
    "flash": dict(tq=512, tk=1024, rb=8, group=4, lag=1, vmem_mib=58),
    "mix": dict(tm=256, vmem_mib=56),
    "mem_kv": dict(tn=1024, vmem_mib=32),
    "mem_attn": dict(tm=512, vmem_mib=48),
    "ffn": dict(tm=512, tn=512, tiles_per_step=2, vmem_mib=58),
}


def _cparams(sem, vmem_mib):
    return pltpu.CompilerParams(dimension_semantics=sem,
                                vmem_limit_bytes=int(vmem_mib * MIB))


def _rms(xf, g):
    ms = jnp.mean(xf * xf, axis=-1, keepdims=True)
    return xf * lax.rsqrt(ms + EPS) * g


def _norm_rows(dst_ref, dst_off, src_ref, g, rows, chunk):
    def body(c, carry):
        r = pl.multiple_of(c * chunk, chunk)
        y = _rms(src_ref[pl.ds(r, chunk), :].astype(F32), g)
        dst_ref[pl.ds(dst_off + r, chunk), :] = y.astype(dst_ref.dtype)
        return carry
    lax.fori_loop(0, rows // chunk, body, 0)


def _with_casts(kernel, n_in, n_out, n_cast):
    def wrapped(*refs):
        ins, rest = refs[:n_in], refs[n_in:]
        cast_in, rest = rest[:n_cast], rest[n_cast:]
        outs, rest = rest[:n_out], rest[n_out:]
        cast_out, scratch = rest[:n_cast], rest[n_cast:]
        for src, dst in zip(cast_in, cast_out):
            dst[...] = src[...].astype(BF16)
        kernel(*ins, *outs, *scratch)
    return wrapped


def _cast_specs(weights, steps):
    specs, shapes = [], []
    for w in weights:
        rows = w.shape[0] // steps
        assert rows * steps == w.shape[0] and rows % BF16_ROWS == 0
        specs.append(pl.BlockSpec((rows, w.shape[1]), lambda i: (i, 0)))
        shapes.append(jax.ShapeDtypeStruct(w.shape, BF16))
    return specs, shapes


def _in_proj_kernel(x_ref, g_ref, wg_ref, wp_ref, wl_ref, zg_ref, zp_ref, zl_ref, h_ref,
                    *, tm, gate_tiles):
    j = pl.program_id(1)

    @pl.when(j == 0)
    def _():
        _norm_rows(h_ref, 0, x_ref, g_ref[...], tm, 128)

    def project(w_ref, z_ref):
        z_ref[...] = lax.dot_general(h_ref[...], w_ref[...], (((1,), (1,)), ((), ())),
                                     preferred_element_type=F32).astype(z_ref.dtype)

    pl.when(j < gate_tiles)(lambda: project(wg_ref, zg_ref))
    pl.when(j == gate_tiles)(lambda: project(wp_ref, zp_ref))
    pl.when(j == gate_tiles + 1)(lambda: project(wl_ref, zl_ref))


def _in_proj(x, g, w_gates, w_pool, w_lat, *, tm, tn, vmem_mib):
    s, d = x.shape
    n_gates, n_pool, n_lat = w_gates.shape[0], w_pool.shape[0], w_lat.shape[0]
    gate_tiles = n_gates // tn

    def gate_tile(j):
        return jnp.minimum(j, gate_tiles - 1)
    return pl.pallas_call(
        functools.partial(_in_proj_kernel, tm=tm, gate_tiles=gate_tiles),
        grid=(s // tm, gate_tiles + 2),
        in_specs=[pl.BlockSpec((tm, d), lambda i, j: (i, 0)),
                  pl.BlockSpec((1, d), lambda i, j: (0, 0)),
                  pl.BlockSpec((tn, d), lambda i, j: (gate_tile(j), 0)),
                  pl.BlockSpec((n_pool, d), lambda i, j: (0, 0), pipeline_mode=pl.Buffered(1)),
                  pl.BlockSpec((n_lat, d), lambda i, j: (0, 0), pipeline_mode=pl.Buffered(1))],
        out_specs=[pl.BlockSpec((tm, tn), lambda i, j: (i, gate_tile(j))),
                   pl.BlockSpec((tm, n_pool), lambda i, j: (i, 0)),
                   pl.BlockSpec((tm, n_lat), lambda i, j: (i, 0))],
        out_shape=[jax.ShapeDtypeStruct((s, n_gates), BF16),
                   jax.ShapeDtypeStruct((s, n_pool), BF16),
                   jax.ShapeDtypeStruct((s, n_lat), BF16)],
        scratch_shapes=[pltpu.VMEM((tm, d), BF16)],
        compiler_params=_cparams(("parallel", "arbitrary"), vmem_mib),
        name="in_proj",
    )(x, g, w_gates, w_pool, w_lat)


def _mla_proj_kernel(ql_ref, ckv_ref, kp_ref, pos_ref, freq_ref, gq_ref, gkv_ref,
                     wq_ref, wkv_ref, q_ref, kn_ref, kpe_ref, v_ref, *, q_scale):
    qn = _rms(ql_ref[...].astype(F32), gq_ref[...]).astype(BF16)
    cn = _rms(ckv_ref[...].astype(F32), gkv_ref[...]).astype(BF16)
    ang = pos_ref[...] * freq_ref[...]
    lane = lax.broadcasted_iota(jnp.int32, (1, 2 * ROPE), 1)
    sign = jnp.where((lane >= ROPE) & (lane < ROPE + ROPE // 2), -1.0, 1.0)
    table = jnp.where(lane < ROPE, jnp.cos(ang), jnp.sin(ang) * sign)
    low = lane < ROPE

    def rot(t):
        t = t * table
        return t + pltpu.roll(t, ROPE, axis=1)

    kpe_ref[...] = jnp.where(low, rot(kp_ref[...].astype(F32)), 0.0).astype(BF16)
    for h in range(N_HEADS):
        qh = jnp.dot(qn, wq_ref[h], preferred_element_type=F32)
        q_ref[h, :, 0:NOPE] = (qh[:, 0:NOPE] * q_scale).astype(BF16)
        q_ref[h, :, NOPE:HEAD_PAD] = (rot(qh[:, NOPE:HEAD_PAD]) * q_scale).astype(BF16)
        kvh = jnp.dot(cn, wkv_ref[h], preferred_element_type=F32)
        kn_ref[h] = kvh[:, 0:NOPE].astype(BF16)
        v_ref[h] = kvh[:, NOPE:HEAD_PAD].astype(BF16)


def _mla_proj(z_lat, pos, freq, gq, gkv, wq, wkv, cast, *, tm, vmem_mib):
    s = z_lat.shape[0]
    q_scale = (NOPE + ROPE) ** -0.5 * math.log2(math.e)
    cast_specs, cast_shapes = _cast_specs(cast, s // tm)

    def head_out(width):
        return jax.ShapeDtypeStruct((N_HEADS, s, width), BF16)

    def head_spec(width):
        return pl.BlockSpec((N_HEADS, tm, width), lambda i: (0, i, 0))
    const3 = lambda i: (0, 0, 0)
    return pl.pallas_call(
        _with_casts(functools.partial(_mla_proj_kernel, q_scale=q_scale), 9, 4, len(cast)),
        grid=(s // tm,),
        in_specs=[pl.BlockSpec((tm, Q_LORA), lambda i: (i, LAT_Q // Q_LORA)),
                  pl.BlockSpec((tm, KV_LORA), lambda i: (i, LAT_KV // KV_LORA)),
                  pl.BlockSpec((tm, 2 * ROPE), lambda i: (i, LAT_KPE // (2 * ROPE))),
                  pl.BlockSpec((tm, 1), lambda i: (i, 0)),
                  pl.BlockSpec((1, 2 * ROPE), lambda i: (0, 0)),
                  pl.BlockSpec((1, Q_LORA), lambda i: (0, 0)),
                  pl.BlockSpec((1, KV_LORA), lambda i: (0, 0)),
                  pl.BlockSpec(wq.shape, const3, pipeline_mode=pl.Buffered(1)),
                  pl.BlockSpec(wkv.shape, const3, pipeline_mode=pl.Buffered(1))]
        + cast_specs,
        out_specs=[head_spec(HEAD_PAD), head_spec(NOPE),
                   pl.BlockSpec((tm, HEAD_PAD - NOPE), lambda i: (i, 0)),
                   head_spec(V_DIM)] + cast_specs,
        out_shape=[head_out(HEAD_PAD), head_out(NOPE),
                   jax.ShapeDtypeStruct((s, HEAD_PAD - NOPE), BF16),
                   head_out(V_DIM)] + cast_shapes,
        compiler_params=_cparams(("parallel",), vmem_mib),
        name="mla_proj",
    )(z_lat, z_lat, z_lat, pos, freq, gq, gkv, wq, wkv, *cast)


M_INIT = float(np.finfo(np.float32).min)
EDGE_BLOCK = 3


def _flash_kernel(q_ref, kn_ref, kpe_ref, v_ref, o_ref, *scratch, tq, tk, rb, group, lag):
    n_s = 2 * lag
    s_bufs, (p0, p1, a0, a1, m_ref, acc_ref) = scratch[:n_s], scratch[n_s:]
    p_bufs, a_bufs = (p0, p1), (a0, a1)
    r = tk // tq
    nb = rb // r
    band = pl.program_id(1)
    row0 = band * rb
    row_end = row0 + rb
    n_plain = r * nb * nb * band + r * (nb * (nb - 1) // 2)
    p1[...] = jnp.zeros(p1.shape, BF16)
    a1[...] = jnp.ones(a1.shape, F32)
    acc_ref[...] = jnp.zeros(acc_ref.shape, F32)
    m_ref[...] = jnp.full(m_ref.shape, M_INIT, F32)

    def last_tile(i):
        return lax.div(i, jnp.int32(r))

    def stage_a(e, row, j, width=tk):
        q = q_ref[0, pl.ds(pl.multiple_of(row * tq, tq), tq), :]
        cols = pl.ds(pl.multiple_of(j * tk, tk), width)
        kt = jnp.concatenate([kn_ref[0, cols, :], kpe_ref[cols, :]], axis=1)
        s_bufs[e % n_s][:, 0:width] = lax.dot_general(q, kt, (((1,), (1,)), ((), ())),
                                                      preferred_element_type=F32)

    def stage_b(e, row, width=tk, off=None):
        s = s_bufs[e % n_s][:, 0:width]
        if off is not None:
            rel = (lax.broadcasted_iota(jnp.int32, (tq, width), 1)
                   - lax.broadcasted_iota(jnp.int32, (tq, width), 0))
            s = jnp.where(rel <= off, s, -jnp.inf)
        m_prev = m_ref[row]
        m_new = jnp.maximum(m_prev, jnp.max(s, axis=-1, keepdims=True))
        a_bufs[e % 2][...] = jnp.exp2(m_prev - m_new)
        p_bufs[e % 2][:, 0:width] = jnp.exp2(s - m_new).astype(BF16)
        m_ref[row] = m_new

    def stage_c(e, row, j, width=tk):
        vt = jnp.concatenate([v_ref[0, pl.ds(pl.multiple_of(j * tk, tk), width), :],
                              jnp.ones((width, HEAD_PAD - V_DIM), BF16)], axis=1)
        acc = a_bufs[e % 2][...] * acc_ref[row] + jnp.dot(
            p_bufs[e % 2][:, 0:width], vt, preferred_element_type=F32)
        acc_ref[row] = acc
        return acc

    def advance(i, j):
        lt = last_tile(i)
        same_row = j + 1 < lt
        more_rows = i + 1 < row_end
        i_plain = jnp.where(same_row, i, jnp.where(more_rows, i + 1, row0))
        j_plain = jnp.where(same_row, j + 1, jnp.where(more_rows, 0, last_tile(row0)))
        on_edge = j == lt
        return (jnp.where(on_edge, i + 1, i_plain),
                jnp.where(on_edge, last_tile(i + 1), j_plain))

    first = jnp.maximum(row0, r)
    pairs = [(first, jnp.int32(0))]
    for _ in range(lag + 1):
        pairs.append(advance(*pairs[-1]))
    for e in range(lag):
        stage_a(e, pairs[e][0] - row0, pairs[e][1])

    def plain_group(_, state):
        state = list(state)
        for g in range(group):
            (ia, ja), (ib, jb), (ic, jc) = state[0], state[lag], state[lag + 1]
            stage_a(lag + g, ia - row0, ja)
            stage_b(g, ib - row0)
            stage_c(g - 1, ic - row0, jc)
            state = [advance(ia, ja)] + state[:-1]
        return tuple(state)

    state = tuple(pairs[lag - d] for d in range(lag + 1)) + (pairs[0],)
    state = lax.fori_loop(0, n_plain // group, plain_group, state)

    ic0, jc0 = state[lag + 1]
    jb0 = last_tile(row0)

    def edge_tick(k):
        if lag <= k + lag < rb:
            ka = k + lag
            stage_a(ka, ka, jb0 + ka // r, (ka % r + 1) * tq)
        if k < rb:
            stage_b(k, k, (k % r + 1) * tq, (k % r) * tq)
        if k == 0:
            stage_c(-1, ic0 - row0, jc0)
        else:
            kc = k - 1
            acc = stage_c(kc, kc, jb0 + kc // r, (kc % r + 1) * tq)
            o_ref[kc * tq:k * tq, :] = (
                acc[:, 0:V_DIM] / acc[:, V_DIM:HEAD_PAD]).astype(o_ref.dtype)

    for k0 in range(0, rb + 1, EDGE_BLOCK):
        @pl.when(band >= 0)
        def _():
            for k in range(k0, min(k0 + EDGE_BLOCK, rb + 1)):
                edge_tick(k)


def _flash(q, kn, kpe, v, *, tq, tk, rb, group, lag, vmem_mib):
    h, s, _ = q.shape
    band = tq * rb
    r = tk // tq
    nb = rb // r
    assert tk % tq == 0 and rb % r == 0 and s % band == 0
    assert group % (2 * lag) == 0 and nb >= 2 and lag <= rb
    assert (r * nb * nb) % group == 0 and (r * (nb * (nb - 1) // 2)) % group == 0
    return pl.pallas_call(
        functools.partial(_flash_kernel, tq=tq, tk=tk, rb=rb, group=group, lag=lag),
        grid=(h, s // band),
        in_specs=[pl.BlockSpec((1, band, HEAD_PAD), lambda hh, b: (hh, b, 0)),
                  pl.BlockSpec((1, s, NOPE), lambda hh, b: (hh, 0, 0)),
                  pl.BlockSpec((s, HEAD_PAD - NOPE), lambda hh, b: (0, 0),
                               pipeline_mode=pl.Buffered(1)),
                  pl.BlockSpec((1, s, V_DIM), lambda hh, b: (hh, 0, 0))],
        out_specs=pl.BlockSpec((band, V_DIM), lambda hh, b: (b, hh)),
        out_shape=jax.ShapeDtypeStruct((s, h * V_DIM), BF16),
        scratch_shapes=[pltpu.VMEM((tq, tk), F32)] * (2 * lag) + [
                        pltpu.VMEM((tq, tk), BF16), pltpu.VMEM((tq, tk), BF16),
                        pltpu.VMEM((tq, 1), F32), pltpu.VMEM((tq, 1), F32),
                        pltpu.VMEM((rb, tq, 1), F32),
                        pltpu.VMEM((rb, tq, HEAD_PAD), F32)],
        compiler_params=_cparams(("parallel", "arbitrary"), vmem_mib),
        name="flash",
    )(q, kn, kpe, v)


def _mix_kernel(o_ref, ga_ref, gp_ref, u_ref, uh_ref, x_ref, wab_ref, wpg_ref, ps_ref,
                wpb_ref, wmo_ref, g_ref, out_ref, ext_ref, yp_ref, *, tm):
    i = pl.program_id(0)
    ya = jnp.dot(o_ref[...], wab_ref[...], preferred_element_type=F32)
    ext_ref[0:HALO, :] = jnp.where(i == 0, 0.0, uh_ref[...].astype(F32))
    ext_ref[HALO:HALO + tm, :] = u_ref[...].astype(F32)
    t = i * tm + lax.broadcasted_iota(jnp.int32, (tm, 1), 0)
    for g, w in enumerate(POOL_WINDOWS):
        cols = slice(g * POOL_GROUP_DIM, (g + 1) * POOL_GROUP_DIM)
        cur = ext_ref[HALO:HALO + tm, cols]
        win = cur
        for b in range(1, w):
            win = win + ext_ref[HALO - b:HALO - b + tm, cols]
        cnt = jnp.minimum(t + 1, w).astype(F32)
        d = win / cnt - cur
        yg = jnp.dot(d.astype(BF16), wpg_ref[g], preferred_element_type=F32)
        yp_ref[:, cols] = (yg * ps_ref[:, cols]).astype(BF16)
    yp = jnp.dot(yp_ref[...], wpb_ref[...], preferred_element_type=F32)
    merged = (jax.nn.sigmoid(ga_ref[...].astype(F32)) * ya
              + jax.nn.sigmoid(gp_ref[...].astype(F32)) * yp)
    mo = jnp.dot(merged.astype(BF16), wmo_ref[...], preferred_element_type=F32)
    out_ref[...] = x_ref[...] + _rms(mo, g_ref[...])


def _resident(shape):
    zeros = (0,) * len(shape)
    return pl.BlockSpec(shape, lambda i: zeros, pipeline_mode=pl.Buffered(1))


def _mix(o, z_gates, z_pool, x, wab, wpg, ps, wpb, wmo, g, cast, *, tm, vmem_mib):
    s, d = x.shape
    cast_specs, cast_shapes = _cast_specs(cast, s // tm)
    return pl.pallas_call(
        _with_casts(functools.partial(_mix_kernel, tm=tm), 12, 1, len(cast)),
        grid=(s // tm,),
        in_specs=[pl.BlockSpec((tm, ATTN_WIDTH), lambda i: (i, 0)),
                  pl.BlockSpec((tm, d), lambda i: (i, 0)),
                  pl.BlockSpec((tm, d), lambda i: (i, 1)),
                  pl.BlockSpec((tm, POOL_WIDTH), lambda i: (i, 0)),
                  pl.BlockSpec((HALO, POOL_WIDTH),
                               lambda i: (jnp.maximum(i * (tm // HALO) - 1, 0), 0)),
                  pl.BlockSpec((tm, d), lambda i: (i, 0)),
                  _resident(wab.shape), _resident(wpg.shape), _resident(ps.shape),
                  _resident(wpb.shape), _resident(wmo.shape), _resident(g.shape)]
        + cast_specs,
        out_specs=[pl.BlockSpec((tm, d), lambda i: (i, 0))] + cast_specs,
        out_shape=[jax.ShapeDtypeStruct((s, d), F32)] + cast_shapes,
        scratch_shapes=[pltpu.VMEM((HALO + tm, POOL_WIDTH), F32),
                        pltpu.VMEM((tm, POOL_WIDTH), BF16)],
        compiler_params=_cparams(("parallel",), vmem_mib),
        name="mix",
    )(o, z_gates, z_gates, z_pool, z_pool, x, wab, wpg, ps, wpb, wmo, g, *cast)


def _mem_kv_kernel(mem_ref, g_ref, w_ref, kv_ref):
    mn = _rms(mem_ref[...], g_ref[...]).astype(BF16)
    kv_ref[...] = jnp.dot(mn, w_ref[...], preferred_element_type=F32).astype(kv_ref.dtype)


def _mem_kv(mem, g, w, *, tn, vmem_mib):
    m, d = mem.shape
    n = w.shape[1]
    return pl.pallas_call(
        _mem_kv_kernel,
        grid=(n // tn,),
        in_specs=[pl.BlockSpec((m, d), lambda j: (0, 0)),
                  pl.BlockSpec((1, d), lambda j: (0, 0)),
                  pl.BlockSpec((d, tn), lambda j: (0, j))],
        out_specs=pl.BlockSpec((m, tn), lambda j: (0, j)),
        out_shape=jax.ShapeDtypeStruct((m, n), BF16),
        compiler_params=_cparams(("parallel",), vmem_mib),
        name="mem_kv",
    )(mem, g, w)


def _mem_attn_kernel(x_ref, gpre_ref, wq_ref, kv_ref, wo_ref, gpost_ref, out_ref,
                     o_ref):
    x = x_ref[...]
    h = _rms(x, gpre_ref[...]).astype(BF16)
    scale = MEM_HEAD_DIM ** -0.5
    q = (jnp.dot(h, wq_ref[...], preferred_element_type=F32) * scale).astype(BF16)
    for hd in range(MEM_HEADS):
        cols = slice(hd * MEM_HEAD_DIM, (hd + 1) * MEM_HEAD_DIM)
        vcols = slice(D_MODEL + hd * MEM_HEAD_DIM, D_MODEL + (hd + 1) * MEM_HEAD_DIM)
        s = lax.dot_general(q[:, cols], kv_ref[:, cols], (((1,), (1,)), ((), ())),
                            preferred_element_type=F32)
        p = jnp.exp(s - jnp.max(s, axis=-1, keepdims=True))
        l = jnp.sum(p, axis=-1, keepdims=True)
        oh = jnp.dot(p.astype(BF16), kv_ref[:, vcols], preferred_element_type=F32)
        o_ref[:, cols] = (oh / l).astype(BF16)
    y = jnp.dot(o_ref[...], wo_ref[...], preferred_element_type=F32)
    out_ref[...] = x + _rms(y, gpost_ref[...])


def _mem_attn(x, gpre, wq, kv, wo, gpost, cast, *, tm, vmem_mib):
    s, d = x.shape
    cast_specs, cast_shapes = _cast_specs(cast, s // tm)
    return pl.pallas_call(
        _with_casts(_mem_attn_kernel, 6, 1, len(cast)),
        grid=(s // tm,),
        in_specs=[pl.BlockSpec((tm, d), lambda i: (i, 0)),
                  _resident(gpre.shape), _resident(wq.shape), _resident(kv.shape),
                  _resident(wo.shape), _resident(gpost.shape)] + cast_specs,
        out_specs=[pl.BlockSpec((tm, d), lambda i: (i, 0))] + cast_specs,
        out_shape=[jax.ShapeDtypeStruct((s, d), F32)] + cast_shapes,
        scratch_shapes=[pltpu.VMEM((tm, d), BF16)],
        compiler_params=_cparams(("parallel",), vmem_mib),
        name="mem_attn",
    )(x, gpre, wq, kv, wo, gpost, *cast)


def _gelu_tanh(x):
    c = math.sqrt(2.0 / math.pi)
    return 0.5 * x * (1.0 + jnp.tanh(c * (x + 0.044715 * (x * x * x))))


FFN_TILE_REFS = 7


def _ffn_kernel(x_ref, xh_ref, gpre_ref, *refs, tm, tiles_per_step, n_tiles):
    n_in = FFN_TILE_REFS * tiles_per_step
    tile_refs = [refs[FFN_TILE_REFS * t:FFN_TILE_REFS * (t + 1)] for t in range(tiles_per_step)]
    gpost_ref, out_ref, h_ref = refs[n_in:n_in + 3]
    a_refs = refs[n_in + 3:]
    i = pl.program_id(0)
    j = pl.program_id(1)

    @pl.when(j == 0)
    def _():
        hh = _rms(xh_ref[...], gpre_ref[...])
        h_ref[0:HALO, :] = jnp.where(i == 0, 0.0, hh).astype(BF16)
        _norm_rows(h_ref, HALO, x_ref, gpre_ref[...], tm, 128)
        out_ref[...] = jnp.zeros(out_ref.shape, F32)

    def conv(a_ref, w_ref, cw_ref, cb_ref):
        a_ref[...] = jnp.dot(h_ref[...], w_ref[...], preferred_element_type=F32)
        return (cb_ref[...]
                + a_ref[HALO - 2:HALO - 2 + tm, :] * cw_ref[0:1, :]
                + a_ref[HALO - 1:HALO - 1 + tm, :] * cw_ref[1:2, :]
                + a_ref[HALO:HALO + tm, :] * cw_ref[2:3, :])

    def tile_update(t):
        wg_ref, wu_ref, cwg_ref, cwu_ref, cbg_ref, cbu_ref, wo_ref = tile_refs[t]
        act = _gelu_tanh(conv(a_refs[t], wg_ref, cwg_ref, cbg_ref))
        up = conv(a_refs[t], wu_ref, cwu_ref, cbu_ref)
        return jnp.dot((act * up).astype(BF16), wo_ref[...], preferred_element_type=F32)

    def step(n):
        upd = tile_update(0)
        for t in range(1, n):
            upd = upd + tile_update(t)
        out_ref[...] += upd

    full_steps, rest = divmod(n_tiles, tiles_per_step)
    pl.when(j < full_steps)(lambda: step(tiles_per_step))
    if rest:
        pl.when(j == full_steps)(lambda: step(rest))

    @pl.when(j == pl.num_programs(1) - 1)
    def _():
        out_ref[...] = x_ref[...] + _rms(out_ref[...], gpost_ref[...])


def _ffn(x, gpre, w_in, conv_w, conv_b, w_out, gpost, *, tm, tn, tiles_per_step, vmem_mib):
    s, d = x.shape
    n_tiles = D_FF // tn

    def tile_specs(t):
        def tile(j):
            return jnp.minimum(j * tiles_per_step + t, n_tiles - 1)
        return [pl.BlockSpec((d, tn), lambda i, j: (0, tile(j))),
                pl.BlockSpec((d, tn), lambda i, j: (0, n_tiles + tile(j))),
                pl.BlockSpec((3, tn), lambda i, j: (0, tile(j))),
                pl.BlockSpec((3, tn), lambda i, j: (0, n_tiles + tile(j))),
                pl.BlockSpec((1, tn), lambda i, j: (0, tile(j))),
                pl.BlockSpec((1, tn), lambda i, j: (0, n_tiles + tile(j))),
                pl.BlockSpec((tn, d), lambda i, j: (tile(j), 0))]
    tile_args = (w_in, w_in, conv_w, conv_w, conv_b, conv_b, w_out)
    return pl.pallas_call(
        functools.partial(_ffn_kernel, tm=tm, tiles_per_step=tiles_per_step,
                          n_tiles=n_tiles),
        grid=(s // tm, pl.cdiv(n_tiles, tiles_per_step)),
        in_specs=[pl.BlockSpec((tm, d), lambda i, j: (i, 0)),
                  pl.BlockSpec((HALO, d),
                               lambda i, j: (jnp.maximum(i * (tm // HALO) - 1, 0), 0)),
                  pl.BlockSpec((1, d), lambda i, j: (0, 0))]
        + [spec for t in range(tiles_per_step) for spec in tile_specs(t)]
        + [pl.BlockSpec((1, d), lambda i, j: (0, 0))],
        out_specs=pl.BlockSpec((tm, d), lambda i, j: (i, 0)),
        out_shape=jax.ShapeDtypeStruct((s, d), F32),
        scratch_shapes=[pltpu.VMEM((HALO + tm, d), BF16)]
        + [pltpu.VMEM((HALO + tm, tn), F32)] * tiles_per_step,
        compiler_params=_cparams(("parallel", "arbitrary"), vmem_mib),
        name="ffn",
    )(x, x, gpre, *(tile_args * tiles_per_step), gpost)


W_IN_LAT_ROWS = Q_LORA + KV_LORA + ROPE
W_IN_GATE_ROW0 = W_IN_LAT_ROWS + POOL_WIDTH


def _prep_w_in_kernel(gates_ref, pool_ref, lat_ref, wg_ref, wp_ref, wl_ref, *, pool_steps):
    i = pl.program_id(0)
    wg_ref[...] = gates_ref[...].astype(BF16)

    @pl.when(i < pool_steps)
    def _():
        wp_ref[...] = pool_ref[...].astype(BF16)

    @pl.when(i == 0)
    def _():
        half = ROPE // 2
        k_pe = W_IN_LAT_ROWS - ROPE
        wl_ref[0:W_IN_LAT_ROWS, :] = lat_ref[...].astype(BF16)
        wl_ref[W_IN_LAT_ROWS:W_IN_LAT_ROWS + half, :] = (
            lat_ref[k_pe + half:W_IN_LAT_ROWS, :].astype(BF16))
        wl_ref[W_IN_LAT_ROWS + half:W_IN_LAT_ROWS + ROPE, :] = (
            lat_ref[k_pe:k_pe + half, :].astype(BF16))
        wl_ref[W_IN_LAT_ROWS + ROPE:LAT_WIDTH, :] = jnp.zeros(
            (LAT_WIDTH - W_IN_LAT_ROWS - ROPE, wl_ref.shape[1]), BF16)


def _prep_w_in(w_in_t, *, rows, vmem_mib):
    n, d = w_in_t.shape
    n_gates = n - W_IN_GATE_ROW0
    steps, pool_steps = n_gates // rows, POOL_WIDTH // rows

    def pool_block(i):
        return jnp.minimum(i, pool_steps - 1)
    return pl.pallas_call(
        functools.partial(_prep_w_in_kernel, pool_steps=pool_steps),
        grid=(steps,),
        in_specs=[pl.BlockSpec((pl.Element(rows), pl.Element(d)),
                               lambda i: (pl.multiple_of(W_IN_GATE_ROW0 + rows * i,
                                                         BF16_ROWS), 0)),
                  pl.BlockSpec((pl.Element(rows), pl.Element(d)),
                               lambda i: (pl.multiple_of(
                                   W_IN_LAT_ROWS + rows * pool_block(i), BF16_ROWS), 0)),
                  pl.BlockSpec((pl.Element(W_IN_LAT_ROWS), pl.Element(d)), lambda i: (0, 0),
                               pipeline_mode=pl.Buffered(1))],
        out_specs=[pl.BlockSpec((rows, d), lambda i: (i, 0)),
                   pl.BlockSpec((rows, d), lambda i: (pool_block(i), 0)),
                   pl.BlockSpec((LAT_WIDTH, d), lambda i: (0, 0))],
        out_shape=[jax.ShapeDtypeStruct((n_gates, d), BF16),
                   jax.ShapeDtypeStruct((POOL_WIDTH, d), BF16),
                   jax.ShapeDtypeStruct((LAT_WIDTH, d), BF16)],
        compiler_params=_cparams(("arbitrary",), vmem_mib),
        name="prep_w_in",
    )(w_in_t, w_in_t, w_in_t)


def _prep_w_heads_kernel(wq_ref, wkv_ref, wq_out, wkv_out):
    half = ROPE // 2
    wq = wq_ref[...]
    for h in range(N_HEADS):
        c = h * (NOPE + ROPE)
        pe = c + NOPE
        wq_out[h] = jnp.concatenate(
            [wq[:, c:pe + ROPE], wq[:, pe + half:pe + ROPE], wq[:, pe:pe + half]],
            axis=1).astype(BF16)
        wkv_out[h] = wkv_ref[:, h * HEAD_PAD:(h + 1) * HEAD_PAD].astype(BF16)


def _prep_w_heads(w_q_b, w_kv_b, *, vmem_mib):
    out = jax.ShapeDtypeStruct((N_HEADS, Q_LORA, HEAD_PAD), BF16)
    return pl.pallas_call(
        _prep_w_heads_kernel,
        out_shape=[out, out],
        compiler_params=pltpu.CompilerParams(vmem_limit_bytes=int(vmem_mib * MIB)),
        name="prep_w_heads",
    )(w_q_b, w_kv_b)


def kernel(x, mem, positions, g_mix_pre, w_in, g_q_a, w_q_b, g_kv_a, w_kv_b, w_attn_br,
           w_pool_grp, pool_scale, w_pool_br, w_mix_out, g_mix_post, g_mem_pre, g_mem_kv,
           w_mem_q, w_mem_kv, w_mem_o, g_mem_post, g_ffn_pre, w_ffn_in, conv_w, conv_b,
           w_ffn_out, g_ffn_post):
    batch, seq, d = x.shape
    depth = w_in.shape[0]
    assert (batch, seq, d) == (1, SEQ, D_MODEL)
    inv_freq = ROPE_THETA ** (-np.arange(0, ROPE, 2, dtype=np.float32) / ROPE)
    freq = jnp.asarray(np.tile(inv_freq.astype(np.float32), 4)[None, :])
    pos = positions.reshape(seq, 1).astype(F32)
    xs = x.reshape(seq, d)
    mems = mem.reshape(N_MEM, d)
    for l in range(depth):
        z_gates, z_pool, z_lat = _in_proj(xs, g_mix_pre[l][None],
                                          *_prep_w_in(w_in[l].T, **CALLS["prep_w_in"]),
                                          **CALLS["in_proj"])
        wq, wkv = _prep_w_heads(w_q_b[l], w_kv_b[l], **CALLS["prep_w_heads"])
        q, kn, kpe, v, wab, wmo, wpb = _mla_proj(
            z_lat, pos, freq, g_q_a[l][None], g_kv_a[l][None], wq, wkv,
            (w_attn_br[l], w_mix_out[l], w_pool_br[l]), **CALLS["mla_proj"])
        o = _flash(q, kn, kpe, v, **CALLS["flash"])
        x1, wfi, wmq, wmo_mem, wmkv = _mix(
            o, z_gates, z_pool, xs, wab, w_pool_grp[l].astype(BF16), pool_scale[l][None],
            wpb, wmo, g_mix_post[l][None],
            (w_ffn_in[l], w_mem_q[l], w_mem_o[l], w_mem_kv[l]), **CALLS["mix"])
        kv = _mem_kv(mems, g_mem_kv[l][None], wmkv, **CALLS["mem_kv"])
        x2, wfo = _mem_attn(x1, g_mem_pre[l][None], wmq, kv, wmo_mem, g_mem_post[l][None],
                            (w_ffn_out[l],), **CALLS["mem_attn"])
        xs = _ffn(x2, g_ffn_pre[l][None], wfi, conv_w[l], conv_b[l][None], wfo,
                  g_ffn_post[l][None], **CALLS["ffn"])
    return xs.reshape(batch, seq, d)
```

```python
import functools
import math

import numpy as np
import jax
import jax.numpy as jnp
from jax import lax
from jax.experimental import pallas as pl
from jax.experimental.pallas import tpu as pltpu

F32 = jnp.float32
BF16 = jnp.bfloat16

D_MODEL = 2048
SEQ = 16384
N_HEADS = 16
Q_LORA = 512
KV_LORA = 512
NOPE = 128
ROPE = 64
V_DIM = 128
ROPE_THETA = 10000.0
ATTN_WIDTH = N_HEADS * V_DIM
POOL_WIDTH = 1024
POOL_WINDOWS = (2, 4, 8, 16)
POOL_GROUP_DIM = 256
N_MEM = 256
MEM_HEADS = 4
MEM_HEAD_DIM = 512
D_FF = 5632
EPS = 1e-6

LAT_Q = 0
LAT_KV = 512
LAT_KPE = 1024
LAT_WIDTH = 1280

HEAD_PAD = 256
HALO = 16
MIB = 1024 * 1024

CALLS = {
    "prep_w_in": dict(rows=512, vmem_mib=48),
    "prep_w_heads": dict(vmem_mib=48),
    "in_proj": dict(tm=1024, tn=1024, vmem_mib=58),
    "mla_proj": dict(tm=512, vmem_mib=48),
    "flash": dict(tq=256, tk=1024, rb=16, group=8, lag=1, vmem_mib=58),
    "mix": dict(tm=256, vmem_mib=56),
    "mem_kv": dict(tn=1024, vmem_mib=32),
    "mem_attn": dict(tm=512, vmem_mib=48),
    "ffn": dict(tm=512, tn=512, tiles_per_step=2, vmem_mib=58),
}


def _cparams(sem, vmem_mib):
    return pltpu.CompilerParams(dimension_semantics=sem,
                                vmem_limit_bytes=int(vmem_mib * MIB))


def _rms(xf, g):
    ms = jnp.mean(xf * xf, axis=-1, keepdims=True)
    return xf * lax.rsqrt(ms + EPS) * g


def _norm_rows(dst_ref, dst_off, src_ref, g, rows, chunk):
    def body(c, carry):
        r = pl.multiple_of(c * chunk, chunk)
        y = _rms(src_ref[pl.ds(r, chunk), :].astype(F32), g)
        dst_ref[pl.ds(dst_off + r, chunk), :] = y.astype(dst_ref.dtype)
        return carry
    lax.fori_loop(0, rows // chunk, body, 0)


def _with_casts(kernel, n_in, n_out, n_cast):
    def wrapped(*refs):
        ins, rest = refs[:n_in], refs[n_in:]
        cast_in, rest = rest[:n_cast], rest[n_cast:]
        outs, rest = rest[:n_out], rest[n_out:]
        cast_out, scratch = rest[:n_cast], rest[n_cast:]
        for src, dst in zip(cast_in, cast_out):
            dst[...] = src[...].astype(BF16)
        kernel(*ins, *outs, *scratch)
    return wrapped


def _cast_specs(weights, steps):
    specs, shapes = [], []
    for w in weights:
        rows = w.shape[0] // steps
        assert rows * steps == w.shape[0] and rows % 16 == 0
        specs.append(pl.BlockSpec((rows, w.shape[1]), lambda i: (i, 0)))
        shapes.append(jax.ShapeDtypeStruct(w.shape, BF16))
    return specs, shapes


def _in_proj_kernel(x_ref, g_ref, wg_ref, wp_ref, wl_ref, zg_ref, zp_ref, zl_ref, h_ref,
                    *, tm, gate_tiles):
    j = pl.program_id(1)

    @pl.when(j == 0)
    def _():
        _norm_rows(h_ref, 0, x_ref, g_ref[...], tm, 128)

    def project(w_ref, z_ref):
        z_ref[...] = lax.dot_general(h_ref[...], w_ref[...], (((1,), (1,)), ((), ())),
                                     preferred_element_type=F32).astype(z_ref.dtype)

    pl.when(j < gate_tiles)(lambda: project(wg_ref, zg_ref))
    pl.when(j == gate_tiles)(lambda: project(wp_ref, zp_ref))
    pl.when(j == gate_tiles + 1)(lambda: project(wl_ref, zl_ref))


def _in_proj(x, g, w_gates, w_pool, w_lat, *, tm, tn, vmem_mib):
    s, d = x.shape
    n_gates, n_pool, n_lat = w_gates.shape[0], w_pool.shape[0], w_lat.shape[0]
    gate_tiles = n_gates // tn

    def gate_tile(j):
        return jnp.minimum(j, gate_tiles - 1)
    return pl.pallas_call(
        functools.partial(_in_proj_kernel, tm=tm, gate_tiles=gate_tiles),
        grid=(s // tm, gate_tiles + 2),
        in_specs=[pl.BlockSpec((tm, d), lambda i, j: (i, 0)),
                  pl.BlockSpec((1, d), lambda i, j: (0, 0)),
                  pl.BlockSpec((tn, d), lambda i, j: (gate_tile(j), 0)),
                  pl.BlockSpec((n_pool, d), lambda i, j: (0, 0), pipeline_mode=pl.Buffered(1)),
                  pl.BlockSpec((n_lat, d), lambda i, j: (0, 0), pipeline_mode=pl.Buffered(1))],
        out_specs=[pl.BlockSpec((tm, tn), lambda i, j: (i, gate_tile(j))),
                   pl.BlockSpec((tm, n_pool), lambda i, j: (i, 0)),
                   pl.BlockSpec((tm, n_lat), lambda i, j: (i, 0))],
        out_shape=[jax.ShapeDtypeStruct((s, n_gates), BF16),
                   jax.ShapeDtypeStruct((s, n_pool), BF16),
                   jax.ShapeDtypeStruct((s, n_lat), BF16)],
        scratch_shapes=[pltpu.VMEM((tm, d), BF16)],
        compiler_params=_cparams(("parallel", "arbitrary"), vmem_mib),
        name="in_proj",
    )(x, g, w_gates, w_pool, w_lat)


def _mla_proj_kernel(ql_ref, ckv_ref, kp_ref, pos_ref, freq_ref, gq_ref, gkv_ref,
                     wq_ref, wkv_ref, q_ref, kn_ref, kpe_ref, v_ref, *, q_scale):
    qn = _rms(ql_ref[...].astype(F32), gq_ref[...]).astype(BF16)
    cn = _rms(ckv_ref[...].astype(F32), gkv_ref[...]).astype(BF16)
    ang = pos_ref[...] * freq_ref[...]
    lane = lax.broadcasted_iota(jnp.int32, (1, 2 * ROPE), 1)
    sign = jnp.where((lane >= ROPE) & (lane < ROPE + ROPE // 2), -1.0, 1.0)
    table = jnp.where(lane < ROPE, jnp.cos(ang), jnp.sin(ang) * sign)
    low = lane < ROPE

    def rot(t):
        t = t * table
        return t + pltpu.roll(t, ROPE, axis=1)

    kpe_ref[...] = jnp.where(low, rot(kp_ref[...].astype(F32)), 0.0).astype(BF16)
    for h in range(N_HEADS):
        qh = jnp.dot(qn, wq_ref[h], preferred_element_type=F32)
        q_ref[h, :, 0:NOPE] = (qh[:, 0:NOPE] * q_scale).astype(BF16)
        q_ref[h, :, NOPE:HEAD_PAD] = (rot(qh[:, NOPE:HEAD_PAD]) * q_scale).astype(BF16)
        kvh = jnp.dot(cn, wkv_ref[h], preferred_element_type=F32)
        kn_ref[h] = kvh[:, 0:NOPE].astype(BF16)
        v_ref[h] = kvh[:, NOPE:HEAD_PAD].astype(BF16)


def _mla_proj(z_lat, pos, freq, gq, gkv, wq, wkv, cast, *, tm, vmem_mib):
    s = z_lat.shape[0]
    q_scale = (NOPE + ROPE) ** -0.5 * math.log2(math.e)
    cast_specs, cast_shapes = _cast_specs(cast, s // tm)

    def head_out(width):
        return jax.ShapeDtypeStruct((N_HEADS, s, width), BF16)

    def head_spec(width):
        return pl.BlockSpec((N_HEADS, tm, width), lambda i: (0, i, 0))
    const3 = lambda i: (0, 0, 0)
    return pl.pallas_call(
        _with_casts(functools.partial(_mla_proj_kernel, q_scale=q_scale), 9, 4, len(cast)),
        grid=(s // tm,),
        in_specs=[pl.BlockSpec((tm, Q_LORA), lambda i: (i, LAT_Q // Q_LORA)),
                  pl.BlockSpec((tm, KV_LORA), lambda i: (i, LAT_KV // KV_LORA)),
                  pl.BlockSpec((tm, 2 * ROPE), lambda i: (i, LAT_KPE // (2 * ROPE))),
                  pl.BlockSpec((tm, 1), lambda i: (i, 0)),
                  pl.BlockSpec((1, 2 * ROPE), lambda i: (0, 0)),
                  pl.BlockSpec((1, Q_LORA), lambda i: (0, 0)),
                  pl.BlockSpec((1, KV_LORA), lambda i: (0, 0)),
                  pl.BlockSpec((N_HEADS, Q_LORA, HEAD_PAD), const3,
                               pipeline_mode=pl.Buffered(1)),
                  pl.BlockSpec((N_HEADS, KV_LORA, HEAD_PAD), const3,
                               pipeline_mode=pl.Buffered(1))] + cast_specs,
        out_specs=[head_spec(HEAD_PAD), head_spec(NOPE),
                   pl.BlockSpec((tm, HEAD_PAD - NOPE), lambda i: (i, 0)),
                   head_spec(V_DIM)] + cast_specs,
        out_shape=[head_out(HEAD_PAD), head_out(NOPE),
                   jax.ShapeDtypeStruct((s, HEAD_PAD - NOPE), BF16),
                   head_out(V_DIM)] + cast_shapes,
        compiler_params=_cparams(("parallel",), vmem_mib),
        name="mla_proj",
    )(z_lat, z_lat, z_lat, pos, freq, gq, gkv, wq, wkv, *cast)


M_INIT = float(np.finfo(np.float32).min)
EDGE_BLOCK = 3


def _flash_kernel(q_ref, kn_ref, kpe_ref, v_ref, o_ref, *scratch, tq, tk, rb, group, lag):
    n_s = 2 * lag
    s_bufs, (p0, p1, a0, a1, m_ref, acc_ref) = scratch[:n_s], scratch[n_s:]
    p_bufs, a_bufs = (p0, p1), (a0, a1)
    r = tk // tq
    nb = rb // r
    band = pl.program_id(1)
    row0 = band * rb
    row_end = row0 + rb
    n_plain = r * nb * nb * band + r * (nb * (nb - 1) // 2)
    p1[...] = jnp.zeros(p1.shape, BF16)
    a1[...] = jnp.ones(a1.shape, F32)
    acc_ref[...] = jnp.zeros(acc_ref.shape, F32)
    m_ref[...] = jnp.full(m_ref.shape, M_INIT, F32)

    def last_tile(i):
        return lax.div(i, jnp.int32(r))

    def stage_a(e, row, j, width=tk):
        q = q_ref[0, pl.ds(pl.multiple_of(row * tq, tq), tq), :]
        cols = pl.ds(pl.multiple_of(j * tk, tk), width)
        kt = jnp.concatenate([kn_ref[0, cols, :], kpe_ref[cols, :]], axis=1)
        s_bufs[e % n_s][:, 0:width] = lax.dot_general(q, kt, (((1,), (1,)), ((), ())),
                                                      preferred_element_type=F32)

    def stage_b(e, row, width=tk, off=None):
        s = s_bufs[e % n_s][:, 0:width]
        if off is not None:
            rel = (lax.broadcasted_iota(jnp.int32, (tq, width), 1)
                   - lax.broadcasted_iota(jnp.int32, (tq, width), 0))
            s = jnp.where(rel <= off, s, -jnp.inf)
        m_prev = m_ref[row]
        m_new = jnp.maximum(m_prev, jnp.max(s, axis=-1, keepdims=True))
        a_bufs[e % 2][...] = jnp.exp2(m_prev - m_new)
        p_bufs[e % 2][:, 0:width] = jnp.exp2(s - m_new).astype(BF16)
        m_ref[row] = m_new

    def stage_c(e, row, j, width=tk):
        vt = jnp.concatenate([v_ref[0, pl.ds(pl.multiple_of(j * tk, tk), width), :],
                              jnp.ones((width, HEAD_PAD - V_DIM), BF16)], axis=1)
        acc = a_bufs[e % 2][...] * acc_ref[row] + jnp.dot(
            p_bufs[e % 2][:, 0:width], vt, preferred_element_type=F32)
        acc_ref[row] = acc
        return acc

    def advance(i, j):
        lt = last_tile(i)
        same_row = j + 1 < lt
        more_rows = i + 1 < row_end
        i_plain = jnp.where(same_row, i, jnp.where(more_rows, i + 1, row0))
        j_plain = jnp.where(same_row, j + 1, jnp.where(more_rows, 0, last_tile(row0)))
        on_edge = j == lt
        return (jnp.where(on_edge, i + 1, i_plain),
                jnp.where(on_edge, last_tile(i + 1), j_plain))

    first = jnp.maximum(row0, r)
    pairs = [(first, jnp.int32(0))]
    for _ in range(lag + 1):
        pairs.append(advance(*pairs[-1]))
    for e in range(lag):
        stage_a(e, pairs[e][0] - row0, pairs[e][1])

    def plain_group(_, state):
        state = list(state)
        for g in range(group):
            (ia, ja), (ib, jb), (ic, jc) = state[0], state[lag], state[lag + 1]
            stage_a(lag + g, ia - row0, ja)
            stage_b(g, ib - row0)
            stage_c(g - 1, ic - row0, jc)
            state = [advance(ia, ja)] + state[:-1]
        return tuple(state)

    state = tuple(pairs[lag - d] for d in range(lag + 1)) + (pairs[0],)
    state = lax.fori_loop(0, n_plain // group, plain_group, state)

    ic0, jc0 = state[lag + 1]
    jb0 = last_tile(row0)

    def edge_tick(k):
        if lag <= k + lag < rb:
            ka = k + lag
            stage_a(ka, ka, jb0 + ka // r, (ka % r + 1) * tq)
        if k < rb:
            stage_b(k, k, (k % r + 1) * tq, (k % r) * tq)
        if k == 0:
            stage_c(-1, ic0 - row0, jc0)
        else:
            kc = k - 1
            acc = stage_c(kc, kc, jb0 + kc // r, (kc % r + 1) * tq)
            o_ref[kc * tq:k * tq, :] = (
                acc[:, 0:V_DIM] / acc[:, V_DIM:HEAD_PAD]).astype(o_ref.dtype)

    for k0 in range(0, rb + 1, EDGE_BLOCK):
        @pl.when(band >= 0)
        def _():
            for k in range(k0, min(k0 + EDGE_BLOCK, rb + 1)):
                edge_tick(k)


def _flash(q, kn, kpe, v, *, tq, tk, rb, group, lag, vmem_mib):
    h, s, _ = q.shape
    band = tq * rb
    r = tk // tq
    nb = rb // r
    assert tk % tq == 0 and rb % r == 0 and s % band == 0
    assert group % (2 * lag) == 0 and nb >= 2 and lag <= rb
    assert (r * nb * nb) % group == 0 and (r * (nb * (nb - 1) // 2)) % group == 0
    return pl.pallas_call(
        functools.partial(_flash_kernel, tq=tq, tk=tk, rb=rb, group=group, lag=lag),
        grid=(h, s // band),
        in_specs=[pl.BlockSpec((1, band, HEAD_PAD), lambda hh, b: (hh, b, 0)),
                  pl.BlockSpec((1, s, NOPE), lambda hh, b: (hh, 0, 0)),
                  pl.BlockSpec((s, HEAD_PAD - NOPE), lambda hh, b: (0, 0),
                               pipeline_mode=pl.Buffered(1)),
                  pl.BlockSpec((1, s, V_DIM), lambda hh, b: (hh, 0, 0))],
        out_specs=pl.BlockSpec((band, V_DIM), lambda hh, b: (b, hh)),
        out_shape=jax.ShapeDtypeStruct((s, h * V_DIM), BF16),
        scratch_shapes=[pltpu.VMEM((tq, tk), F32)] * (2 * lag) + [
                        pltpu.VMEM((tq, tk), BF16), pltpu.VMEM((tq, tk), BF16),
                        pltpu.VMEM((tq, 1), F32), pltpu.VMEM((tq, 1), F32),
                        pltpu.VMEM((rb, tq, 1), F32),
                        pltpu.VMEM((rb, tq, HEAD_PAD), F32)],
        compiler_params=_cparams(("parallel", "arbitrary"), vmem_mib),
        name="flash",
    )(q, kn, kpe, v)


def _mix_kernel(o_ref, ga_ref, gp_ref, u_ref, uh_ref, x_ref, wab_ref, wpg_ref, ps_ref,
                wpb_ref, wmo_ref, g_ref, out_ref, ext_ref, yp_ref, *, tm):
    i = pl.program_id(0)
    ya = jnp.dot(o_ref[...], wab_ref[...], preferred_element_type=F32)
    ext_ref[0:HALO, :] = jnp.where(i == 0, 0.0, uh_ref[...].astype(F32))
    ext_ref[HALO:HALO + tm, :] = u_ref[...].astype(F32)
    t = i * tm + lax.broadcasted_iota(jnp.int32, (tm, 1), 0)
    for g, w in enumerate(POOL_WINDOWS):
        cols = slice(g * POOL_GROUP_DIM, (g + 1) * POOL_GROUP_DIM)
        cur = ext_ref[HALO:HALO + tm, cols]
        win = cur
        for b in range(1, w):
            win = win + ext_ref[HALO - b:HALO - b + tm, cols]
        cnt = jnp.minimum(t + 1, w).astype(F32)
        d = win / cnt - cur
        yg = jnp.dot(d.astype(BF16), wpg_ref[g], preferred_element_type=F32)
        yp_ref[:, cols] = (yg * ps_ref[:, cols]).astype(BF16)
    yp = jnp.dot(yp_ref[...], wpb_ref[...], preferred_element_type=F32)
    merged = (jax.nn.sigmoid(ga_ref[...].astype(F32)) * ya
              + jax.nn.sigmoid(gp_ref[...].astype(F32)) * yp)
    mo = jnp.dot(merged.astype(BF16), wmo_ref[...], preferred_element_type=F32)
    out_ref[...] = x_ref[...] + _rms(mo, g_ref[...])


def _resident(shape):
    zeros = (0,) * len(shape)
    return pl.BlockSpec(shape, lambda i: zeros, pipeline_mode=pl.Buffered(1))


def _mix(o, z_gates, z_pool, x, wab, wpg, ps, wpb, wmo, g, cast, *, tm, vmem_mib):
    s, d = x.shape
    cast_specs, cast_shapes = _cast_specs(cast, s // tm)
    return pl.pallas_call(
        _with_casts(functools.partial(_mix_kernel, tm=tm), 12, 1, len(cast)),
        grid=(s // tm,),
        in_specs=[pl.BlockSpec((tm, ATTN_WIDTH), lambda i: (i, 0)),
                  pl.BlockSpec((tm, d), lambda i: (i, 0)),
                  pl.BlockSpec((tm, d), lambda i: (i, 1)),
                  pl.BlockSpec((tm, POOL_WIDTH), lambda i: (i, 0)),
                  pl.BlockSpec((HALO, POOL_WIDTH),
                               lambda i: (jnp.maximum(i * (tm // HALO) - 1, 0), 0)),
                  pl.BlockSpec((tm, d), lambda i: (i, 0)),
                  _resident(wab.shape), _resident(wpg.shape), _resident(ps.shape),
                  _resident(wpb.shape), _resident(wmo.shape), _resident(g.shape)]
        + cast_specs,
        out_specs=[pl.BlockSpec((tm, d), lambda i: (i, 0))] + cast_specs,
        out_shape=[jax.ShapeDtypeStruct((s, d), F32)] + cast_shapes,
        scratch_shapes=[pltpu.VMEM((HALO + tm, POOL_WIDTH), F32),
                        pltpu.VMEM((tm, POOL_WIDTH), BF16)],
        compiler_params=_cparams(("parallel",), vmem_mib),
        name="mix",
    )(o, z_gates, z_gates, z_pool, z_pool, x, wab, wpg, ps, wpb, wmo, g, *cast)


def _mem_kv_kernel(mem_ref, g_ref, w_ref, kv_ref):
    mn = _rms(mem_ref[...], g_ref[...]).astype(BF16)
    kv_ref[...] = jnp.dot(mn, w_ref[...], preferred_element_type=F32).astype(kv_ref.dtype)


def _mem_kv(mem, g, w, *, tn, vmem_mib):
    m, d = mem.shape
    n = w.shape[1]
    return pl.pallas_call(
        _mem_kv_kernel,
        grid=(n // tn,),
        in_specs=[pl.BlockSpec((m, d), lambda j: (0, 0)),
                  pl.BlockSpec((1, d), lambda j: (0, 0)),
                  pl.BlockSpec((d, tn), lambda j: (0, j))],
        out_specs=pl.BlockSpec((m, tn), lambda j: (0, j)),
        out_shape=jax.ShapeDtypeStruct((m, n), BF16),
        compiler_params=_cparams(("parallel",), vmem_mib),
        name="mem_kv",
    )(mem, g, w)


def _mem_attn_kernel(x_ref, gpre_ref, wq_ref, kv_ref, wo_ref, gpost_ref, out_ref,
                     o_ref):
    x = x_ref[...]
    h = _rms(x, gpre_ref[...]).astype(BF16)
    scale = MEM_HEAD_DIM ** -0.5
    q = (jnp.dot(h, wq_ref[...], preferred_element_type=F32) * scale).astype(BF16)
    for hd in range(MEM_HEADS):
        cols = slice(hd * MEM_HEAD_DIM, (hd + 1) * MEM_HEAD_DIM)
        vcols = slice(D_MODEL + hd * MEM_HEAD_DIM, D_MODEL + (hd + 1) * MEM_HEAD_DIM)
        s = lax.dot_general(q[:, cols], kv_ref[:, cols], (((1,), (1,)), ((), ())),
                            preferred_element_type=F32)
        p = jnp.exp(s - jnp.max(s, axis=-1, keepdims=True))
        l = jnp.sum(p, axis=-1, keepdims=True)
        oh = jnp.dot(p.astype(BF16), kv_ref[:, vcols], preferred_element_type=F32)
        o_ref[:, cols] = (oh / l).astype(BF16)
    y = jnp.dot(o_ref[...], wo_ref[...], preferred_element_type=F32)
    out_ref[...] = x + _rms(y, gpost_ref[...])


def _mem_attn(x, gpre, wq, kv, wo, gpost, cast, *, tm, vmem_mib):
    s, d = x.shape
    cast_specs, cast_shapes = _cast_specs(cast, s // tm)
    return pl.pallas_call(
        _with_casts(_mem_attn_kernel, 6, 1, len(cast)),
        grid=(s // tm,),
        in_specs=[pl.BlockSpec((tm, d), lambda i: (i, 0)),
                  _resident(gpre.shape), _resident(wq.shape), _resident(kv.shape),
                  _resident(wo.shape), _resident(gpost.shape)] + cast_specs,
        out_specs=[pl.BlockSpec((tm, d), lambda i: (i, 0))] + cast_specs,
        out_shape=[jax.ShapeDtypeStruct((s, d), F32)] + cast_shapes,
        scratch_shapes=[pltpu.VMEM((tm, d), BF16)],
        compiler_params=_cparams(("parallel",), vmem_mib),
        name="mem_attn",
    )(x, gpre, wq, kv, wo, gpost, *cast)


def _gelu_tanh(x):
    c = math.sqrt(2.0 / math.pi)
    return 0.5 * x * (1.0 + jnp.tanh(c * (x + 0.044715 * (x * x * x))))


FFN_TILE_REFS = 7


def _ffn_kernel(x_ref, xh_ref, gpre_ref, *refs, tm, tiles_per_step, n_tiles):
    n_in = FFN_TILE_REFS * tiles_per_step
    tile_refs = [refs[FFN_TILE_REFS * t:FFN_TILE_REFS * (t + 1)] for t in range(tiles_per_step)]
    gpost_ref, out_ref, h_ref = refs[n_in:n_in + 3]
    a_refs = refs[n_in + 3:]
    i = pl.program_id(0)
    j = pl.program_id(1)

    @pl.when(j == 0)
    def _():
        hh = _rms(xh_ref[...], gpre_ref[...])
        h_ref[0:HALO, :] = jnp.where(i == 0, 0.0, hh).astype(BF16)
        _norm_rows(h_ref, HALO, x_ref, gpre_ref[...], tm, 128)
        out_ref[...] = jnp.zeros(out_ref.shape, F32)

    def conv(a_ref, w_ref, cw_ref, cb_ref):
        a_ref[...] = jnp.dot(h_ref[...], w_ref[...], preferred_element_type=F32)
        return (cb_ref[...]
                + a_ref[HALO - 2:HALO - 2 + tm, :] * cw_ref[0:1, :]
                + a_ref[HALO - 1:HALO - 1 + tm, :] * cw_ref[1:2, :]
                + a_ref[HALO:HALO + tm, :] * cw_ref[2:3, :])

    def tile_update(t):
        wg_ref, wu_ref, cwg_ref, cwu_ref, cbg_ref, cbu_ref, wo_ref = tile_refs[t]
        act = _gelu_tanh(conv(a_refs[t], wg_ref, cwg_ref, cbg_ref))
        up = conv(a_refs[t], wu_ref, cwu_ref, cbu_ref)
        return jnp.dot((act * up).astype(BF16), wo_ref[...], preferred_element_type=F32)

    def step(n):
        upd = tile_update(0)
        for t in range(1, n):
            upd = upd + tile_update(t)
        out_ref[...] += upd

    full_steps, rest = divmod(n_tiles, tiles_per_step)
    pl.when(j < full_steps)(lambda: step(tiles_per_step))
    if rest:
        pl.when(j == full_steps)(lambda: step(rest))

    @pl.when(j == pl.num_programs(1) - 1)
    def _():
        out_ref[...] = x_ref[...] + _rms(out_ref[...], gpost_ref[...])


def _ffn(x, gpre, w_in, conv_w, conv_b, w_out, gpost, *, tm, tn, tiles_per_step, vmem_mib):
    s, d = x.shape
    n_tiles = D_FF // tn

    def tile_specs(t):
        def tile(j):
            return jnp.minimum(j * tiles_per_step + t, n_tiles - 1)
        return [pl.BlockSpec((d, tn), lambda i, j: (0, tile(j))),
                pl.BlockSpec((d, tn), lambda i, j: (0, n_tiles + tile(j))),
                pl.BlockSpec((3, tn), lambda i, j: (0, tile(j))),
                pl.BlockSpec((3, tn), lambda i, j: (0, n_tiles + tile(j))),
                pl.BlockSpec((1, tn), lambda i, j: (0, tile(j))),
                pl.BlockSpec((1, tn), lambda i, j: (0, n_tiles + tile(j))),
                pl.BlockSpec((tn, d), lambda i, j: (tile(j), 0))]
    tile_args = (w_in, w_in, conv_w, conv_w, conv_b, conv_b, w_out)
    return pl.pallas_call(
        functools.partial(_ffn_kernel, tm=tm, tiles_per_step=tiles_per_step,
                          n_tiles=n_tiles),
        grid=(s // tm, pl.cdiv(n_tiles, tiles_per_step)),
        in_specs=[pl.BlockSpec((tm, d), lambda i, j: (i, 0)),
                  pl.BlockSpec((HALO, d),
                               lambda i, j: (jnp.maximum(i * (tm // HALO) - 1, 0), 0)),
                  pl.BlockSpec((1, d), lambda i, j: (0, 0))]
        + [spec for t in range(tiles_per_step) for spec in tile_specs(t)]
        + [pl.BlockSpec((1, d), lambda i, j: (0, 0))],
        out_specs=pl.BlockSpec((tm, d), lambda i, j: (i, 0)),
        out_shape=jax.ShapeDtypeStruct((s, d), F32),
        scratch_shapes=[pltpu.VMEM((HALO + tm, d), BF16)]
        + [pltpu.VMEM((HALO + tm, tn), F32)] * tiles_per_step,
        compiler_params=_cparams(("parallel", "arbitrary"), vmem_mib),
        name="ffn",
    )(x, x, gpre, *(tile_args * tiles_per_step), gpost)


W_IN_LAT_ROWS = Q_LORA + KV_LORA + ROPE
W_IN_GATE_ROW0 = W_IN_LAT_ROWS + POOL_WIDTH


def _prep_w_in_kernel(gates_ref, pool_ref, lat_ref, wg_ref, wp_ref, wl_ref, *, pool_steps):
    i = pl.program_id(0)
    wg_ref[...] = gates_ref[...].astype(BF16)

    @pl.when(i < pool_steps)
    def _():
        wp_ref[...] = pool_ref[...].astype(BF16)

    @pl.when(i == 0)
    def _():
        half = ROPE // 2
        k_pe = W_IN_LAT_ROWS - ROPE
        wl_ref[0:W_IN_LAT_ROWS, :] = lat_ref[...].astype(BF16)
        wl_ref[W_IN_LAT_ROWS:W_IN_LAT_ROWS + half, :] = (
            lat_ref[k_pe + half:W_IN_LAT_ROWS, :].astype(BF16))
        wl_ref[W_IN_LAT_ROWS + half:W_IN_LAT_ROWS + ROPE, :] = (
            lat_ref[k_pe:k_pe + half, :].astype(BF16))
        wl_ref[W_IN_LAT_ROWS + ROPE:LAT_WIDTH, :] = jnp.zeros(
            (LAT_WIDTH - W_IN_LAT_ROWS - ROPE, wl_ref.shape[1]), BF16)


def _prep_w_in(w_in_t, *, rows, vmem_mib):
    n, d = w_in_t.shape
    n_gates = n - W_IN_GATE_ROW0
    steps, pool_steps = n_gates // rows, POOL_WIDTH // rows

    def pool_block(i):
        return jnp.minimum(i, pool_steps - 1)
    return pl.pallas_call(
        functools.partial(_prep_w_in_kernel, pool_steps=pool_steps),
        grid=(steps,),
        in_specs=[pl.BlockSpec((pl.Element(rows), pl.Element(d)),
                               lambda i: (pl.multiple_of(W_IN_GATE_ROW0 + rows * i, HALO), 0)),
                  pl.BlockSpec((pl.Element(rows), pl.Element(d)),
                               lambda i: (pl.multiple_of(
                                   W_IN_LAT_ROWS + rows * pool_block(i), HALO), 0)),
                  pl.BlockSpec((pl.Element(W_IN_LAT_ROWS), pl.Element(d)), lambda i: (0, 0),
                               pipeline_mode=pl.Buffered(1))],
        out_specs=[pl.BlockSpec((rows, d), lambda i: (i, 0)),
                   pl.BlockSpec((rows, d), lambda i: (pool_block(i), 0)),
                   pl.BlockSpec((LAT_WIDTH, d), lambda i: (0, 0))],
        out_shape=[jax.ShapeDtypeStruct((n_gates, d), BF16),
                   jax.ShapeDtypeStruct((POOL_WIDTH, d), BF16),
                   jax.ShapeDtypeStruct((LAT_WIDTH, d), BF16)],
        compiler_params=_cparams(("arbitrary",), vmem_mib),
        name="prep_w_in",
    )(w_in_t, w_in_t, w_in_t)


def _prep_w_heads_kernel(wq_ref, wkv_ref, wq_out, wkv_out):
    half = ROPE // 2
    wq = wq_ref[...]
    for h in range(N_HEADS):
        c = h * (NOPE + ROPE)
        pe = c + NOPE
        wq_out[h] = jnp.concatenate(
            [wq[:, c:pe + ROPE], wq[:, pe + half:pe + ROPE], wq[:, pe:pe + half]],
            axis=1).astype(BF16)
        wkv_out[h] = wkv_ref[:, h * HEAD_PAD:(h + 1) * HEAD_PAD].astype(BF16)


def _prep_w_heads(w_q_b, w_kv_b, *, vmem_mib):
    out = jax.ShapeDtypeStruct((N_HEADS, Q_LORA, HEAD_PAD), BF16)
    return pl.pallas_call(
        _prep_w_heads_kernel,
        out_shape=[out, out],
        compiler_params=pltpu.CompilerParams(vmem_limit_bytes=int(vmem_mib * MIB)),
        name="prep_w_heads",
    )(w_q_b, w_kv_b)


def kernel(x, mem, positions, g_mix_pre, w_in, g_q_a, w_q_b, g_kv_a, w_kv_b, w_attn_br,
           w_pool_grp, pool_scale, w_pool_br, w_mix_out, g_mix_post, g_mem_pre, g_mem_kv,
           w_mem_q, w_mem_kv, w_mem_o, g_mem_post, g_ffn_pre, w_ffn_in, conv_w, conv_b,
           w_ffn_out, g_ffn_post):
    batch, seq, d = x.shape
    depth = w_in.shape[0]
    assert (batch, seq, d) == (1, SEQ, D_MODEL)
    inv_freq = ROPE_THETA ** (-np.arange(0, ROPE, 2, dtype=np.float32) / ROPE)
    freq = jnp.asarray(np.tile(inv_freq.astype(np.float32), 4)[None, :])
    pos = positions.reshape(seq, 1).astype(F32)
    xs = x.reshape(seq, d)
    mems = mem.reshape(N_MEM, d)
    for l in range(depth):
        z_gates, z_pool, z_lat = _in_proj(xs, g_mix_pre[l][None],
                                          *_prep_w_in(w_in[l].T, **CALLS["prep_w_in"]),
                                          **CALLS["in_proj"])
        wq, wkv = _prep_w_heads(w_q_b[l], w_kv_b[l], **CALLS["prep_w_heads"])
        q, kn, kpe, v, wab, wmo, wpb = _mla_proj(
            z_lat, pos, freq, g_q_a[l][None], g_kv_a[l][None], wq, wkv,
            (w_attn_br[l], w_mix_out[l], w_pool_br[l]), **CALLS["mla_proj"])
        o = _flash(q, kn, kpe, v, **CALLS["flash"])
        x1, wfi, wmq, wmo_mem, wmkv = _mix(
            o, z_gates, z_pool, xs, wab, w_pool_grp[l].astype(BF16), pool_scale[l][None],
            wpb, wmo, g_mix_post[l][None],
            (w_ffn_in[l], w_mem_q[l], w_mem_o[l], w_mem_kv[l]), **CALLS["mix"])
        kv = _mem_kv(mems, g_mem_kv[l][None], wmkv, **CALLS["mem_kv"])
        x2, wfo = _mem_attn(x1, g_mem_pre[l][None], wmq, kv, wmo_mem, g_mem_post[l][None],
                            (w_ffn_out[l],), **CALLS["mem_attn"])
        xs = _ffn(x2, g_ffn_pre[l][None], wfi, conv_w[l], conv_b[l][None], wfo,
                  g_ffn_post[l][None], **CALLS["ffn"])
    return xs.reshape(batch, seq, d)
```

```python
import functools
import math

import numpy as np
import jax
import jax.numpy as jnp
from jax import lax
from jax.experimental import pallas as pl
from jax.experimental.pallas import tpu as pltpu

F32 = jnp.float32
BF16 = jnp.bfloat16

D_MODEL = 2048
SEQ = 16384
N_HEADS = 16
Q_LORA = 512
KV_LORA = 512
NOPE = 128
ROPE = 64
V_DIM = 128
ROPE_THETA = 10000.0
ATTN_WIDTH = N_HEADS * V_DIM
POOL_WIDTH = 1024
POOL_WINDOWS = (2, 4, 8, 16)
POOL_GROUP_DIM = 256
N_MEM = 256
MEM_HEADS = 4
MEM_HEAD_DIM = 512
D_FF = 5632
EPS = 1e-6

LAT_Q = 0
LAT_KV = 512
LAT_KPE = 1024
LAT_WIDTH = 1280

HEAD_PAD = 256
HALO = 16
MIB = 1024 * 1024

CALLS = {
    "prep_w_in": dict(rows=512, vmem_mib=48),
    "prep_w_heads": dict(vmem_mib=48),
    "in_proj": dict(tm=1024, tn=1024, vmem_mib=58),
    "mla_proj": dict(tm=512, vmem_mib=48),
    "flash": dict(tq=512, tk=1024, rb=8, group=4, lag=2, vmem_mib=58),
    "mix": dict(tm=256, vmem_mib=56),
    "mem_kv": dict(tn=1024, vmem_mib=32),
    "mem_attn": dict(tm=512, vmem_mib=48),
    "ffn": dict(tm=512, tn=512, tiles_per_step=2, vmem_mib=58),
}


def _cparams(sem, vmem_mib):
    return pltpu.CompilerParams(dimension_semantics=sem,
                                vmem_limit_bytes=int(vmem_mib * MIB))


def _rms(xf, g):
    ms = jnp.mean(xf * xf, axis=-1, keepdims=True)
    return xf * lax.rsqrt(ms + EPS) * g


def _norm_rows(dst_ref, dst_off, src_ref, g, rows, chunk):
    def body(c, carry):
        r = pl.multiple_of(c * chunk, chunk)
        y = _rms(src_ref[pl.ds(r, chunk), :].astype(F32), g)
        dst_ref[pl.ds(dst_off + r, chunk), :] = y.astype(dst_ref.dtype)
        return carry
    lax.fori_loop(0, rows // chunk, body, 0)


def _with_casts(kernel, n_in, n_out, n_cast):
    def wrapped(*refs):
        ins, rest = refs[:n_in], refs[n_in:]
        cast_in, rest = rest[:n_cast], rest[n_cast:]
        outs, rest = rest[:n_out], rest[n_out:]
        cast_out, scratch = rest[:n_cast], rest[n_cast:]
        for src, dst in zip(cast_in, cast_out):
            dst[...] = src[...].astype(BF16)
        kernel(*ins, *outs, *scratch)
    return wrapped


def _cast_specs(weights, steps):
    specs, shapes = [], []
    for w in weights:
        rows = w.shape[0] // steps
        assert rows * steps == w.shape[0] and rows % 16 == 0
        specs.append(pl.BlockSpec((rows, w.shape[1]), lambda i: (i, 0)))
        shapes.append(jax.ShapeDtypeStruct(w.shape, BF16))
    return specs, shapes


def _in_proj_kernel(x_ref, g_ref, wg_ref, wp_ref, wl_ref, zg_ref, zp_ref, zl_ref, h_ref,
                    *, tm, gate_tiles):
    j = pl.program_id(1)

    @pl.when(j == 0)
    def _():
        _norm_rows(h_ref, 0, x_ref, g_ref[...], tm, 128)

    def project(w_ref, z_ref):
        z_ref[...] = lax.dot_general(h_ref[...], w_ref[...], (((1,), (1,)), ((), ())),
                                     preferred_element_type=F32).astype(z_ref.dtype)

    pl.when(j < gate_tiles)(lambda: project(wg_ref, zg_ref))
    pl.when(j == gate_tiles)(lambda: project(wp_ref, zp_ref))
    pl.when(j == gate_tiles + 1)(lambda: project(wl_ref, zl_ref))


def _in_proj(x, g, w_gates, w_pool, w_lat, *, tm, tn, vmem_mib):
    s, d = x.shape
    n_gates, n_pool, n_lat = w_gates.shape[0], w_pool.shape[0], w_lat.shape[0]
    gate_tiles = n_gates // tn

    def gate_tile(j):
        return jnp.minimum(j, gate_tiles - 1)
    return pl.pallas_call(
        functools.partial(_in_proj_kernel, tm=tm, gate_tiles=gate_tiles),
        grid=(s // tm, gate_tiles + 2),
        in_specs=[pl.BlockSpec((tm, d), lambda i, j: (i, 0)),
                  pl.BlockSpec((1, d), lambda i, j: (0, 0)),
                  pl.BlockSpec((tn, d), lambda i, j: (gate_tile(j), 0)),
                  pl.BlockSpec((n_pool, d), lambda i, j: (0, 0), pipeline_mode=pl.Buffered(1)),
                  pl.BlockSpec((n_lat, d), lambda i, j: (0, 0), pipeline_mode=pl.Buffered(1))],
        out_specs=[pl.BlockSpec((tm, tn), lambda i, j: (i, gate_tile(j))),
                   pl.BlockSpec((tm, n_pool), lambda i, j: (i, 0)),
                   pl.BlockSpec((tm, n_lat), lambda i, j: (i, 0))],
        out_shape=[jax.ShapeDtypeStruct((s, n_gates), BF16),
                   jax.ShapeDtypeStruct((s, n_pool), BF16),
                   jax.ShapeDtypeStruct((s, n_lat), BF16)],
        scratch_shapes=[pltpu.VMEM((tm, d), BF16)],
        compiler_params=_cparams(("parallel", "arbitrary"), vmem_mib),
        name="in_proj",
    )(x, g, w_gates, w_pool, w_lat)


def _mla_proj_kernel(ql_ref, ckv_ref, kp_ref, pos_ref, freq_ref, gq_ref, gkv_ref,
                     wq_ref, wkv_ref, q_ref, kn_ref, kpe_ref, v_ref, *, q_scale):
    qn = _rms(ql_ref[...].astype(F32), gq_ref[...]).astype(BF16)
    cn = _rms(ckv_ref[...].astype(F32), gkv_ref[...]).astype(BF16)
    ang = pos_ref[...] * freq_ref[...]
    lane = lax.broadcasted_iota(jnp.int32, (1, 2 * ROPE), 1)
    sign = jnp.where((lane >= ROPE) & (lane < ROPE + ROPE // 2), -1.0, 1.0)
    table = jnp.where(lane < ROPE, jnp.cos(ang), jnp.sin(ang) * sign)
    low = lane < ROPE

    def rot(t):
        t = t * table
        return t + pltpu.roll(t, ROPE, axis=1)

    kpe_ref[...] = jnp.where(low, rot(kp_ref[...].astype(F32)), 0.0).astype(BF16)
    for h in range(N_HEADS):
        qh = jnp.dot(qn, wq_ref[h], preferred_element_type=F32)
        q_ref[h, :, 0:NOPE] = (qh[:, 0:NOPE] * q_scale).astype(BF16)
        q_ref[h, :, NOPE:HEAD_PAD] = (rot(qh[:, NOPE:HEAD_PAD]) * q_scale).astype(BF16)
        kvh = jnp.dot(cn, wkv_ref[h], preferred_element_type=F32)
        kn_ref[h] = kvh[:, 0:NOPE].astype(BF16)
        v_ref[h] = kvh[:, NOPE:HEAD_PAD].astype(BF16)


def _mla_proj(z_lat, pos, freq, gq, gkv, wq, wkv, cast, *, tm, vmem_mib):
    s = z_lat.shape[0]
    q_scale = (NOPE + ROPE) ** -0.5 * math.log2(math.e)
    cast_specs, cast_shapes = _cast_specs(cast, s // tm)

    def head_out(width):
        return jax.ShapeDtypeStruct((N_HEADS, s, width), BF16)

    def head_spec(width):
        return pl.BlockSpec((N_HEADS, tm, width), lambda i: (0, i, 0))
    const3 = lambda i: (0, 0, 0)
    return pl.pallas_call(
        _with_casts(functools.partial(_mla_proj_kernel, q_scale=q_scale), 9, 4, len(cast)),
        grid=(s // tm,),
        in_specs=[pl.BlockSpec((tm, Q_LORA), lambda i: (i, LAT_Q // Q_LORA)),
                  pl.BlockSpec((tm, KV_LORA), lambda i: (i, LAT_KV // KV_LORA)),
                  pl.BlockSpec((tm, 2 * ROPE), lambda i: (i, LAT_KPE // (2 * ROPE))),
                  pl.BlockSpec((tm, 1), lambda i: (i, 0)),
                  pl.BlockSpec((1, 2 * ROPE), lambda i: (0, 0)),
                  pl.BlockSpec((1, Q_LORA), lambda i: (0, 0)),
                  pl.BlockSpec((1, KV_LORA), lambda i: (0, 0)),
                  pl.BlockSpec((N_HEADS, Q_LORA, HEAD_PAD), const3,
                               pipeline_mode=pl.Buffered(1)),
                  pl.BlockSpec((N_HEADS, KV_LORA, HEAD_PAD), const3,
                               pipeline_mode=pl.Buffered(1))] + cast_specs,
        out_specs=[head_spec(HEAD_PAD), head_spec(NOPE),
                   pl.BlockSpec((tm, HEAD_PAD - NOPE), lambda i: (i, 0)),
                   head_spec(V_DIM)] + cast_specs,
        out_shape=[head_out(HEAD_PAD), head_out(NOPE),
                   jax.ShapeDtypeStruct((s, HEAD_PAD - NOPE), BF16),
                   head_out(V_DIM)] + cast_shapes,
        compiler_params=_cparams(("parallel",), vmem_mib),
        name="mla_proj",
    )(z_lat, z_lat, z_lat, pos, freq, gq, gkv, wq, wkv, *cast)


M_INIT = float(np.finfo(np.float32).min)
EDGE_BLOCK = 3


def _flash_kernel(q_ref, kn_ref, kpe_ref, v_ref, o_ref, *scratch, tq, tk, rb, group, lag):
    n_s = 2 * lag
    s_bufs, (p0, p1, a0, a1, m_ref, acc_ref) = scratch[:n_s], scratch[n_s:]
    p_bufs, a_bufs = (p0, p1), (a0, a1)
    r = tk // tq
    nb = rb // r
    band = pl.program_id(1)
    row0 = band * rb
    row_end = row0 + rb
    n_plain = r * nb * nb * band + r * (nb * (nb - 1) // 2)
    p1[...] = jnp.zeros(p1.shape, BF16)
    a1[...] = jnp.ones(a1.shape, F32)
    acc_ref[...] = jnp.zeros(acc_ref.shape, F32)
    m_ref[...] = jnp.full(m_ref.shape, M_INIT, F32)

    def last_tile(i):
        return lax.div(i, jnp.int32(r))

    def stage_a(e, row, j, width=tk):
        q = q_ref[0, pl.ds(pl.multiple_of(row * tq, tq), tq), :]
        cols = pl.ds(pl.multiple_of(j * tk, tk), width)
        kt = jnp.concatenate([kn_ref[0, cols, :], kpe_ref[cols, :]], axis=1)
        s_bufs[e % n_s][:, 0:width] = lax.dot_general(q, kt, (((1,), (1,)), ((), ())),
                                                      preferred_element_type=F32)

    def stage_b(e, row, width=tk, off=None):
        s = s_bufs[e % n_s][:, 0:width]
        if off is not None:
            rel = (lax.broadcasted_iota(jnp.int32, (tq, width), 1)
                   - lax.broadcasted_iota(jnp.int32, (tq, width), 0))
            s = jnp.where(rel <= off, s, -jnp.inf)
        m_prev = m_ref[row]
        m_new = jnp.maximum(m_prev, jnp.max(s, axis=-1, keepdims=True))
        a_bufs[e % 2][...] = jnp.exp2(m_prev - m_new)
        p_bufs[e % 2][:, 0:width] = jnp.exp2(s - m_new).astype(BF16)
        m_ref[row] = m_new

    def stage_c(e, row, j, width=tk):
        vt = jnp.concatenate([v_ref[0, pl.ds(pl.multiple_of(j * tk, tk), width), :],
                              jnp.ones((width, HEAD_PAD - V_DIM), BF16)], axis=1)
        acc = a_bufs[e % 2][...] * acc_ref[row] + jnp.dot(
            p_bufs[e % 2][:, 0:width], vt, preferred_element_type=F32)
        acc_ref[row] = acc
        return acc

    def advance(i, j):
        lt = last_tile(i)
        same_row = j + 1 < lt
        more_rows = i + 1 < row_end
        i_plain = jnp.where(same_row, i, jnp.where(more_rows, i + 1, row0))
        j_plain = jnp.where(same_row, j + 1, jnp.where(more_rows, 0, last_tile(row0)))
        on_edge = j == lt
        return (jnp.where(on_edge, i + 1, i_plain),
                jnp.where(on_edge, last_tile(i + 1), j_plain))

    first = jnp.maximum(row0, r)
    pairs = [(first, jnp.int32(0))]
    for _ in range(lag + 1):
        pairs.append(advance(*pairs[-1]))
    for e in range(lag):
        stage_a(e, pairs[e][0] - row0, pairs[e][1])

    def plain_group(_, state):
        state = list(state)
        for g in range(group):
            (ia, ja), (ib, jb), (ic, jc) = state[0], state[lag], state[lag + 1]
            stage_a(lag + g, ia - row0, ja)
            stage_b(g, ib - row0)
            stage_c(g - 1, ic - row0, jc)
            state = [advance(ia, ja)] + state[:-1]
        return tuple(state)

    state = tuple(pairs[lag - d] for d in range(lag + 1)) + (pairs[0],)
    state = lax.fori_loop(0, n_plain // group, plain_group, state)

    ic0, jc0 = state[lag + 1]
    jb0 = last_tile(row0)

    def edge_tick(k):
        if lag <= k + lag < rb:
            ka = k + lag
            stage_a(ka, ka, jb0 + ka // r, (ka % r + 1) * tq)
        if k < rb:
            stage_b(k, k, (k % r + 1) * tq, (k % r) * tq)
        if k == 0:
            stage_c(-1, ic0 - row0, jc0)
        else:
            kc = k - 1
            acc = stage_c(kc, kc, jb0 + kc // r, (kc % r + 1) * tq)
            o_ref[kc * tq:k * tq, :] = (
                acc[:, 0:V_DIM] / acc[:, V_DIM:HEAD_PAD]).astype(o_ref.dtype)

    for k0 in range(0, rb + 1, EDGE_BLOCK):
        @pl.when(band >= 0)
        def _():
            for k in range(k0, min(k0 + EDGE_BLOCK, rb + 1)):
                edge_tick(k)


def _flash(q, kn, kpe, v, *, tq, tk, rb, group, lag, vmem_mib):
    h, s, _ = q.shape
    band = tq * rb
    r = tk // tq
    nb = rb // r
    assert tk % tq == 0 and rb % r == 0 and s % band == 0
    assert group % (2 * lag) == 0 and nb >= 2 and lag <= rb
    assert (r * nb * nb) % group == 0 and (r * (nb * (nb - 1) // 2)) % group == 0
    return pl.pallas_call(
        functools.partial(_flash_kernel, tq=tq, tk=tk, rb=rb, group=group, lag=lag),
        grid=(h, s // band),
        in_specs=[pl.BlockSpec((1, band, HEAD_PAD), lambda hh, b: (hh, b, 0)),
                  pl.BlockSpec((1, s, NOPE), lambda hh, b: (hh, 0, 0)),
                  pl.BlockSpec((s, HEAD_PAD - NOPE), lambda hh, b: (0, 0),
                               pipeline_mode=pl.Buffered(1)),
                  pl.BlockSpec((1, s, V_DIM), lambda hh, b: (hh, 0, 0))],
        out_specs=pl.BlockSpec((band, V_DIM), lambda hh, b: (b, hh)),
        out_shape=jax.ShapeDtypeStruct((s, h * V_DIM), BF16),
        scratch_shapes=[pltpu.VMEM((tq, tk), F32)] * (2 * lag) + [
                        pltpu.VMEM((tq, tk), BF16), pltpu.VMEM((tq, tk), BF16),
                        pltpu.VMEM((tq, 1), F32), pltpu.VMEM((tq, 1), F32),
                        pltpu.VMEM((rb, tq, 1), F32),
                        pltpu.VMEM((rb, tq, HEAD_PAD), F32)],
        compiler_params=_cparams(("parallel", "arbitrary"), vmem_mib),
        name="flash",
    )(q, kn, kpe, v)


def _mix_kernel(o_ref, ga_ref, gp_ref, u_ref, uh_ref, x_ref, wab_ref, wpg_ref, ps_ref,
                wpb_ref, wmo_ref, g_ref, out_ref, ext_ref, yp_ref, *, tm):
    i = pl.program_id(0)
    ya = jnp.dot(o_ref[...], wab_ref[...], preferred_element_type=F32)
    ext_ref[0:HALO, :] = jnp.where(i == 0, 0.0, uh_ref[...].astype(F32))
    ext_ref[HALO:HALO + tm, :] = u_ref[...].astype(F32)
    t = i * tm + lax.broadcasted_iota(jnp.int32, (tm, 1), 0)
    for g, w in enumerate(POOL_WINDOWS):
        cols = slice(g * POOL_GROUP_DIM, (g + 1) * POOL_GROUP_DIM)
        cur = ext_ref[HALO:HALO + tm, cols]
        win = cur
        for b in range(1, w):
            win = win + ext_ref[HALO - b:HALO - b + tm, cols]
        cnt = jnp.minimum(t + 1, w).astype(F32)
        d = win / cnt - cur
        yg = jnp.dot(d.astype(BF16), wpg_ref[g], preferred_element_type=F32)
        yp_ref[:, cols] = (yg * ps_ref[:, cols]).astype(BF16)
    yp = jnp.dot(yp_ref[...], wpb_ref[...], preferred_element_type=F32)
    merged = (jax.nn.sigmoid(ga_ref[...].astype(F32)) * ya
              + jax.nn.sigmoid(gp_ref[...].astype(F32)) * yp)
    mo = jnp.dot(merged.astype(BF16), wmo_ref[...], preferred_element_type=F32)
    out_ref[...] = x_ref[...] + _rms(mo, g_ref[...])


def _resident(shape):
    zeros = (0,) * len(shape)
    return pl.BlockSpec(shape, lambda i: zeros, pipeline_mode=pl.Buffered(1))


def _mix(o, z_gates, z_pool, x, wab, wpg, ps, wpb, wmo, g, cast, *, tm, vmem_mib):
    s, d = x.shape
    cast_specs, cast_shapes = _cast_specs(cast, s // tm)
    return pl.pallas_call(
        _with_casts(functools.partial(_mix_kernel, tm=tm), 12, 1, len(cast)),
        grid=(s // tm,),
        in_specs=[pl.BlockSpec((tm, ATTN_WIDTH), lambda i: (i, 0)),
                  pl.BlockSpec((tm, d), lambda i: (i, 0)),
                  pl.BlockSpec((tm, d), lambda i: (i, 1)),
                  pl.BlockSpec((tm, POOL_WIDTH), lambda i: (i, 0)),
                  pl.BlockSpec((HALO, POOL_WIDTH),
                               lambda i: (jnp.maximum(i * (tm // HALO) - 1, 0), 0)),
                  pl.BlockSpec((tm, d), lambda i: (i, 0)),
                  _resident(wab.shape), _resident(wpg.shape), _resident(ps.shape),
                  _resident(wpb.shape), _resident(wmo.shape), _resident(g.shape)]
        + cast_specs,
        out_specs=[pl.BlockSpec((tm, d), lambda i: (i, 0))] + cast_specs,
        out_shape=[jax.ShapeDtypeStruct((s, d), F32)] + cast_shapes,
        scratch_shapes=[pltpu.VMEM((HALO + tm, POOL_WIDTH), F32),
                        pltpu.VMEM((tm, POOL_WIDTH), BF16)],
        compiler_params=_cparams(("parallel",), vmem_mib),
        name="mix",
    )(o, z_gates, z_gates, z_pool, z_pool, x, wab, wpg, ps, wpb, wmo, g, *cast)


def _mem_kv_kernel(mem_ref, g_ref, w_ref, kv_ref):
    mn = _rms(mem_ref[...], g_ref[...]).astype(BF16)
    kv_ref[...] = jnp.dot(mn, w_ref[...], preferred_element_type=F32).astype(kv_ref.dtype)


def _mem_kv(mem, g, w, *, tn, vmem_mib):
    m, d = mem.shape
    n = w.shape[1]
    return pl.pallas_call(
        _mem_kv_kernel,
        grid=(n // tn,),
        in_specs=[pl.BlockSpec((m, d), lambda j: (0, 0)),
                  pl.BlockSpec((1, d), lambda j: (0, 0)),
                  pl.BlockSpec((d, tn), lambda j: (0, j))],
        out_specs=pl.BlockSpec((m, tn), lambda j: (0, j)),
        out_shape=jax.ShapeDtypeStruct((m, n), BF16),
        compiler_params=_cparams(("parallel",), vmem_mib),
        name="mem_kv",
    )(mem, g, w)


def _mem_attn_kernel(x_ref, gpre_ref, wq_ref, kv_ref, wo_ref, gpost_ref, out_ref,
                     o_ref):
    x = x_ref[...]
    h = _rms(x, gpre_ref[...]).astype(BF16)
    scale = MEM_HEAD_DIM ** -0.5
    q = (jnp.dot(h, wq_ref[...], preferred_element_type=F32) * scale).astype(BF16)
    for hd in range(MEM_HEADS):
        cols = slice(hd * MEM_HEAD_DIM, (hd + 1) * MEM_HEAD_DIM)
        vcols = slice(D_MODEL + hd * MEM_HEAD_DIM, D_MODEL + (hd + 1) * MEM_HEAD_DIM)
        s = lax.dot_general(q[:, cols], kv_ref[:, cols], (((1,), (1,)), ((), ())),
                            preferred_element_type=F32)
        p = jnp.exp(s - jnp.max(s, axis=-1, keepdims=True))
        l = jnp.sum(p, axis=-1, keepdims=True)
        oh = jnp.dot(p.astype(BF16), kv_ref[:, vcols], preferred_element_type=F32)
        o_ref[:, cols] = (oh / l).astype(BF16)
    y = jnp.dot(o_ref[...], wo_ref[...], preferred_element_type=F32)
    out_ref[...] = x + _rms(y, gpost_ref[...])


def _mem_attn(x, gpre, wq, kv, wo, gpost, cast, *, tm, vmem_mib):
    s, d = x.shape
    cast_specs, cast_shapes = _cast_specs(cast, s // tm)
    return pl.pallas_call(
        _with_casts(_mem_attn_kernel, 6, 1, len(cast)),
        grid=(s // tm,),
        in_specs=[pl.BlockSpec((tm, d), lambda i: (i, 0)),
                  _resident(gpre.shape), _resident(wq.shape), _resident(kv.shape),
                  _resident(wo.shape), _resident(gpost.shape)] + cast_specs,
        out_specs=[pl.BlockSpec((tm, d), lambda i: (i, 0))] + cast_specs,
        out_shape=[jax.ShapeDtypeStruct((s, d), F32)] + cast_shapes,
        scratch_shapes=[pltpu.VMEM((tm, d), BF16)],
        compiler_params=_cparams(("parallel",), vmem_mib),
        name="mem_attn",
    )(x, gpre, wq, kv, wo, gpost, *cast)


def _gelu_tanh(x):
    c = math.sqrt(2.0 / math.pi)
    return 0.5 * x * (1.0 + jnp.tanh(c * (x + 0.044715 * (x * x * x))))


FFN_TILE_REFS = 7


def _ffn_kernel(x_ref, xh_ref, gpre_ref, *refs, tm, tiles_per_step, n_tiles):
    n_in = FFN_TILE_REFS * tiles_per_step
    tile_refs = [refs[FFN_TILE_REFS * t:FFN_TILE_REFS * (t + 1)] for t in range(tiles_per_step)]
    gpost_ref, out_ref, h_ref = refs[n_in:n_in + 3]
    a_refs = refs[n_in + 3:]
    i = pl.program_id(0)
    j = pl.program_id(1)

    @pl.when(j == 0)
    def _():
        hh = _rms(xh_ref[...], gpre_ref[...])
        h_ref[0:HALO, :] = jnp.where(i == 0, 0.0, hh).astype(BF16)
        _norm_rows(h_ref, HALO, x_ref, gpre_ref[...], tm, 128)
        out_ref[...] = jnp.zeros(out_ref.shape, F32)

    def conv(a_ref, w_ref, cw_ref, cb_ref):
        a_ref[...] = jnp.dot(h_ref[...], w_ref[...], preferred_element_type=F32)
        return (cb_ref[...]
                + a_ref[HALO - 2:HALO - 2 + tm, :] * cw_ref[0:1, :]
                + a_ref[HALO - 1:HALO - 1 + tm, :] * cw_ref[1:2, :]
                + a_ref[HALO:HALO + tm, :] * cw_ref[2:3, :])

    def tile_update(t):
        wg_ref, wu_ref, cwg_ref, cwu_ref, cbg_ref, cbu_ref, wo_ref = tile_refs[t]
        act = _gelu_tanh(conv(a_refs[t], wg_ref, cwg_ref, cbg_ref))
        up = conv(a_refs[t], wu_ref, cwu_ref, cbu_ref)
        return jnp.dot((act * up).astype(BF16), wo_ref[...], preferred_element_type=F32)

    def step(n):
        upd = tile_update(0)
        for t in range(1, n):
            upd = upd + tile_update(t)
        out_ref[...] += upd

    full_steps, rest = divmod(n_tiles, tiles_per_step)
    pl.when(j < full_steps)(lambda: step(tiles_per_step))
    if rest:
        pl.when(j == full_steps)(lambda: step(rest))

    @pl.when(j == pl.num_programs(1) - 1)
    def _():
        out_ref[...] = x_ref[...] + _rms(out_ref[...], gpost_ref[...])


def _ffn(x, gpre, w_in, conv_w, conv_b, w_out, gpost, *, tm, tn, tiles_per_step, vmem_mib):
    s, d = x.shape
    n_tiles = D_FF // tn

    def tile_specs(t):
        def tile(j):
            return jnp.minimum(j * tiles_per_step + t, n_tiles - 1)
        return [pl.BlockSpec((d, tn), lambda i, j: (0, tile(j))),
                pl.BlockSpec((d, tn), lambda i, j: (0, n_tiles + tile(j))),
                pl.BlockSpec((3, tn), lambda i, j: (0, tile(j))),
                pl.BlockSpec((3, tn), lambda i, j: (0, n_tiles + tile(j))),
                pl.BlockSpec((1, tn), lambda i, j: (0, tile(j))),
                pl.BlockSpec((1, tn), lambda i, j: (0, n_tiles + tile(j))),
                pl.BlockSpec((tn, d), lambda i, j: (tile(j), 0))]
    tile_args = (w_in, w_in, conv_w, conv_w, conv_b, conv_b, w_out)
    return pl.pallas_call(
        functools.partial(_ffn_kernel, tm=tm, tiles_per_step=tiles_per_step,
                          n_tiles=n_tiles),
        grid=(s // tm, pl.cdiv(n_tiles, tiles_per_step)),
        in_specs=[pl.BlockSpec((tm, d), lambda i, j: (i, 0)),
                  pl.BlockSpec((HALO, d),
                               lambda i, j: (jnp.maximum(i * (tm // HALO) - 1, 0), 0)),
                  pl.BlockSpec((1, d), lambda i, j: (0, 0))]
        + [spec for t in range(tiles_per_step) for spec in tile_specs(t)]
        + [pl.BlockSpec((1, d), lambda i, j: (0, 0))],
        out_specs=pl.BlockSpec((tm, d), lambda i, j: (i, 0)),
        out_shape=jax.ShapeDtypeStruct((s, d), F32),
        scratch_shapes=[pltpu.VMEM((HALO + tm, d), BF16)]
        + [pltpu.VMEM((HALO + tm, tn), F32)] * tiles_per_step,
        compiler_params=_cparams(("parallel", "arbitrary"), vmem_mib),
        name="ffn",
    )(x, x, gpre, *(tile_args * tiles_per_step), gpost)


W_IN_LAT_ROWS = Q_LORA + KV_LORA + ROPE
W_IN_GATE_ROW0 = W_IN_LAT_ROWS + POOL_WIDTH


def _prep_w_in_kernel(gates_ref, pool_ref, lat_ref, wg_ref, wp_ref, wl_ref, *, pool_steps):
    i = pl.program_id(0)
    wg_ref[...] = gates_ref[...].astype(BF16)

    @pl.when(i < pool_steps)
    def _():
        wp_ref[...] = pool_ref[...].astype(BF16)

    @pl.when(i == 0)
    def _():
        half = ROPE // 2
        k_pe = W_IN_LAT_ROWS - ROPE
        wl_ref[0:W_IN_LAT_ROWS, :] = lat_ref[...].astype(BF16)
        wl_ref[W_IN_LAT_ROWS:W_IN_LAT_ROWS + half, :] = (
            lat_ref[k_pe + half:W_IN_LAT_ROWS, :].astype(BF16))
        wl_ref[W_IN_LAT_ROWS + half:W_IN_LAT_ROWS + ROPE, :] = (
            lat_ref[k_pe:k_pe + half, :].astype(BF16))
        wl_ref[W_IN_LAT_ROWS + ROPE:LAT_WIDTH, :] = jnp.zeros(
            (LAT_WIDTH - W_IN_LAT_ROWS - ROPE, wl_ref.shape[1]), BF16)


def _prep_w_in(w_in_t, *, rows, vmem_mib):
    n, d = w_in_t.shape
    n_gates = n - W_IN_GATE_ROW0
    steps, pool_steps = n_gates // rows, POOL_WIDTH // rows

    def pool_block(i):
        return jnp.minimum(i, pool_steps - 1)
    return pl.pallas_call(
        functools.partial(_prep_w_in_kernel, pool_steps=pool_steps),
        grid=(steps,),
        in_specs=[pl.BlockSpec((pl.Element(rows), pl.Element(d)),
                               lambda i: (pl.multiple_of(W_IN_GATE_ROW0 + rows * i, HALO), 0)),
                  pl.BlockSpec((pl.Element(rows), pl.Element(d)),
                               lambda i: (pl.multiple_of(
                                   W_IN_LAT_ROWS + rows * pool_block(i), HALO), 0)),
                  pl.BlockSpec((pl.Element(W_IN_LAT_ROWS), pl.Element(d)), lambda i: (0, 0),
                               pipeline_mode=pl.Buffered(1))],
        out_specs=[pl.BlockSpec((rows, d), lambda i: (i, 0)),
                   pl.BlockSpec((rows, d), lambda i: (pool_block(i), 0)),
                   pl.BlockSpec((LAT_WIDTH, d), lambda i: (0, 0))],
        out_shape=[jax.ShapeDtypeStruct((n_gates, d), BF16),
                   jax.ShapeDtypeStruct((POOL_WIDTH, d), BF16),
                   jax.ShapeDtypeStruct((LAT_WIDTH, d), BF16)],
        compiler_params=_cparams(("arbitrary",), vmem_mib),
        name="prep_w_in",
    )(w_in_t, w_in_t, w_in_t)


def _prep_w_heads_kernel(wq_ref, wkv_ref, wq_out, wkv_out):
    half = ROPE // 2
    wq = wq_ref[...]
    for h in range(N_HEADS):
        c = h * (NOPE + ROPE)
        pe = c + NOPE
        wq_out[h] = jnp.concatenate(
            [wq[:, c:pe + ROPE], wq[:, pe + half:pe + ROPE], wq[:, pe:pe + half]],
            axis=1).astype(BF16)
        wkv_out[h] = wkv_ref[:, h * HEAD_PAD:(h + 1) * HEAD_PAD].astype(BF16)


def _prep_w_heads(w_q_b, w_kv_b, *, vmem_mib):
    out = jax.ShapeDtypeStruct((N_HEADS, Q_LORA, HEAD_PAD), BF16)
    return pl.pallas_call(
        _prep_w_heads_kernel,
        out_shape=[out, out],
        compiler_params=pltpu.CompilerParams(vmem_limit_bytes=int(vmem_mib * MIB)),
        name="prep_w_heads",
    )(w_q_b, w_kv_b)


def kernel(x, mem, positions, g_mix_pre, w_in, g_q_a, w_q_b, g_kv_a, w_kv_b, w_attn_br,
           w_pool_grp, pool_scale, w_pool_br, w_mix_out, g_mix_post, g_mem_pre, g_mem_kv,
           w_mem_q, w_mem_kv, w_mem_o, g_mem_post, g_ffn_pre, w_ffn_in, conv_w, conv_b,
           w_ffn_out, g_ffn_post):
    batch, seq, d = x.shape
    depth = w_in.shape[0]
    assert (batch, seq, d) == (1, SEQ, D_MODEL)
    inv_freq = ROPE_THETA ** (-np.arange(0, ROPE, 2, dtype=np.float32) / ROPE)
    freq = jnp.asarray(np.tile(inv_freq.astype(np.float32), 4)[None, :])
    pos = positions.reshape(seq, 1).astype(F32)
    xs = x.reshape(seq, d)
    mems = mem.reshape(N_MEM, d)
    for l in range(depth):
        z_gates, z_pool, z_lat = _in_proj(xs, g_mix_pre[l][None],
                                          *_prep_w_in(w_in[l].T, **CALLS["prep_w_in"]),
                                          **CALLS["in_proj"])
        wq, wkv = _prep_w_heads(w_q_b[l], w_kv_b[l], **CALLS["prep_w_heads"])
        q, kn, kpe, v, wab, wmo, wpb = _mla_proj(
            z_lat, pos, freq, g_q_a[l][None], g_kv_a[l][None], wq, wkv,
            (w_attn_br[l], w_mix_out[l], w_pool_br[l]), **CALLS["mla_proj"])
        o = _flash(q, kn, kpe, v, **CALLS["flash"])
        x1, wfi, wmq, wmo_mem, wmkv = _mix(
            o, z_gates, z_pool, xs, wab, w_pool_grp[l].astype(BF16), pool_scale[l][None],
            wpb, wmo, g_mix_post[l][None],
            (w_ffn_in[l], w_mem_q[l], w_mem_o[l], w_mem_kv[l]), **CALLS["mix"])
        kv = _mem_kv(mems, g_mem_kv[l][None], wmkv, **CALLS["mem_kv"])
        x2, wfo = _mem_attn(x1, g_mem_pre[l][None], wmq, kv, wmo_mem, g_mem_post[l][None],
                            (w_ffn_out[l],), **CALLS["mem_attn"])
        xs = _ffn(x2, g_ffn_pre[l][None], wfi, conv_w[l], conv_b[l][None], wfo,
                  g_ffn_post[l][None], **CALLS["ffn"])
    return xs.reshape(batch, seq, d)
```
